```python
import jax, jax.numpy as jnp
from jax import lax
import numpy as np

D_MODEL = 1024
BATCH = 16
SEQ = 2048
DEPTH = 4

CTX_LEN = 256
GRID_W = 64

N_BRANCH = 4
BRANCH_W = D_MODEL // 4
GLA_HEADS = 4
GLA_DV = BRANCH_W // GLA_HEADS
GLA_DK = GLA_DV // 2
QK_W = GLA_HEADS * GLA_DK
V_W = GLA_HEADS * GLA_DV
GLA_CHUNK = 64
GK_RANK = 16
GK_TAU = 16.0
POOL_WINDOWS = (2, 4, 8, 16)
POOL_GROUP = BRANCH_W // 4
POOL_W = BRANCH_W
FFT_GROUPS = 4
FFT_GROUP = BRANCH_W // FFT_GROUPS
FFT_W = BRANCH_W
SGU_GROUPS = 4
SGU_GROUP = BRANCH_W // SGU_GROUPS
SGU_CHUNK = 128
SGU_W = BRANCH_W
IN_WIDTHS = (QK_W, QK_W, V_W, V_W, GK_RANK, GK_RANK, POOL_W, FFT_W, SGU_W, SGU_W)
N_IN = sum(IN_WIDTHS)
D_FF = 2816
N_EXPERTS = 8
TOP_K = 2
ALPHA = (2 * DEPTH) ** 0.25
BETA = (8 * DEPTH) ** -0.25
EPS = 1e-6

kernel_name = "hybrid_gla_pool_fourier_sgu_moe_dit"


def layer_norm(x, g=None, b=None):
    xf = x.astype(jnp.float32)
    mu = jnp.mean(xf, -1, keepdims=True)
    var = jnp.mean(jnp.square(xf - mu), -1, keepdims=True)
    y = ((xf - mu) * lax.rsqrt(var + EPS)).astype(x.dtype)
    if g is not None:
        y = y * g + b
    return y


def modulate(x, shift, scale):
    return layer_norm(x) * (1 + scale) + shift


def split_cols(p):
    offsets = [int(o) for o in np.cumsum(IN_WIDTHS)[:-1]]
    return jnp.split(p, offsets, axis=-1)


def split_heads(a, n):
    bsz, t, w = a.shape
    return a.reshape(bsz, t, n, w // n).transpose(0, 2, 1, 3)


def gla_chunked(q, k, v, log_a, h0, inclusive):
    bsz, nh, t, _ = q.shape
    dv = v.shape[-1]
    n = t // GLA_CHUNK
    rs = lambda a: a.reshape(bsz, nh, n, GLA_CHUNK, a.shape[-1])
    q, k, v, log_a = rs(q), rs(k), rs(v), rs(log_a)
    b = jnp.cumsum(log_a, axis=3)
    b_last = b[:, :, :, -1:, :]
    q_e = q * jnp.exp(b)
    k_e = k * jnp.exp(-b)
    k_tail = k * jnp.exp(b_last - b)
    mask = jnp.tril(jnp.ones((GLA_CHUNK, GLA_CHUNK), bool), 0 if inclusive else -1)
    att = jnp.where(mask, jnp.einsum('bhncd,bhnsd->bhncs', q_e, k_e), 0.0)
    o_intra = jnp.einsum('bhncs,bhnsv->bhncv', att, v)
    dh = jnp.einsum('bhncd,bhncv->bhndv', k_tail, v)
    decay = jnp.exp(b_last[:, :, :, 0, :])

    def step(h, inp):
        dec, d = inp
        return dec[..., None] * h + d, h

    h_final, h_prev = lax.scan(step, h0, (jnp.moveaxis(decay, 2, 0), jnp.moveaxis(dh, 2, 0)))
    h_prev = jnp.moveaxis(h_prev, 0, 2)
    o_inter = jnp.einsum('bhncd,bhndv->bhncv', q_e, h_prev)
    return (o_intra + o_inter).reshape(bsz, nh, t, dv), h_final


def gla_direction(q, k, v, log_a, h0, reverse):
    if reverse:
        fl = lambda a: jnp.flip(a, 2)
        o, h = gla_chunked(fl(q), fl(k), fl(v), fl(log_a), h0, False)
        return fl(o), h
    return gla_chunked(q, k, v, log_a, h0, True)


def gla_final_state(k, v, log_a):
    b = jnp.cumsum(log_a, axis=2)
    w = jnp.exp(b[:, :, -1:, :] - b)
    return jnp.einsum('bhtd,bhtv->bhdv', k * w, v)


def gla_inputs(q, k, v, lf, lb, w_gk2, b_gk):
    f32 = jnp.float32
    q = split_heads(q.astype(f32), GLA_HEADS) * GLA_DK ** -0.5
    k = split_heads(k.astype(f32), GLA_HEADS)
    v = split_heads(v.astype(f32), GLA_HEADS)
    la_f = split_heads(jax.nn.log_sigmoid((lf @ w_gk2[0] + b_gk[0]).astype(f32)) / GK_TAU, GLA_HEADS)
    la_b = split_heads(jax.nn.log_sigmoid((lb @ w_gk2[1] + b_gk[1]).astype(f32)) / GK_TAU, GLA_HEADS)
    return q, k, v, la_f, la_b


def gla_output(o_f, o_b, r, g):
    o = (o_f + o_b).transpose(0, 2, 1, 3)
    o = o * lax.rsqrt(jnp.mean(jnp.square(o), -1, keepdims=True) + EPS)
    bsz, t = o.shape[:2]
    return o.reshape(bsz, t, V_W).astype(r.dtype) * g * jax.nn.silu(r)


def window_mean(x, win, axis):
    n = x.shape[axis]
    pad_shape = list(x.shape)
    pad_shape[axis] = 1
    prefix = jnp.concatenate([jnp.zeros(pad_shape, x.dtype), jnp.cumsum(x, axis=axis)], axis=axis)
    pos = jnp.arange(n)
    lo = jnp.clip(pos - win // 2, 0, n)
    hi = jnp.clip(pos + (win - win // 2), 0, n)
    total = jnp.take(prefix, hi, axis=axis) - jnp.take(prefix, lo, axis=axis)
    cnt_shape = [1] * x.ndim
    cnt_shape[axis] = n
    return total / (hi - lo).astype(x.dtype).reshape(cnt_shape)


def pool_branch(xp, pool_w, pool_scale, rows):
    bsz, t, w = xp.shape
    xf = xp.astype(jnp.float32)
    if rows is not None:
        xf = xf.reshape(bsz, rows, GRID_W, w)
    outs = []
    for gi, win in enumerate(POOL_WINDOWS):
        xg = xf[..., gi * POOL_GROUP:(gi + 1) * POOL_GROUP]
        m = window_mean(xg, win, 1)
        if rows is not None:
            m = window_mean(m, win, 2)
        outs.append(jnp.einsum('...c,cd->...d', (m - xg).astype(xp.dtype), pool_w[gi]))
    return jnp.concatenate(outs, -1).reshape(bsz, t, w) * pool_scale


def fourier_branch(xf_in):
    bsz, t, w = xf_in.shape
    z = xf_in.astype(jnp.float32).reshape(bsz, t, FFT_GROUPS, FFT_GROUP)
    y = jnp.fft.fft2(z, axes=(1, 3), norm="ortho").real
    return y.reshape(bsz, t, w).astype(xf_in.dtype)


def sgu_branch(u, v, ln_g, ln_b, sgu_w, sgu_b):
    bsz, t, w = u.shape
    u = jax.nn.gelu(u)
    v = layer_norm(jax.nn.gelu(v), ln_g, ln_b)
    vb = v.reshape(bsz, t // SGU_CHUNK, SGU_CHUNK, SGU_GROUPS, SGU_GROUP)
    s = jnp.einsum('gst,bntgc->bnsgc', sgu_w, vb) + sgu_b.T[None, None, :, :, None]
    return u * s.reshape(bsz, t, w)


def mixer_out(h, cols, o_f, o_b, rows, gla_g, pool_w, pool_scale, sgu_ln_g, sgu_ln_b, sgu_w, sgu_b,
              w_gate, w_branch, w_o):
    ys = (gla_output(o_f, o_b, cols[3], gla_g),
          pool_branch(cols[6], pool_w, pool_scale, rows),
          fourier_branch(cols[7]),
          sgu_branch(cols[8], cols[9], sgu_ln_g, sgu_ln_b, sgu_w, sgu_b))
    m = jax.nn.sigmoid(h @ w_gate[0]) * (ys[0] @ w_branch[0])
    for i in range(1, N_BRANCH):
        m = m + jax.nn.sigmoid(h @ w_gate[i]) * (ys[i] @ w_branch[i])
    return m @ w_o


def token_mix(h_ctx, h_lat, rows, w_in, w_gk2, b_gk, gla_g, pool_w, pool_scale, sgu_ln_g, sgu_ln_b,
              sgu_w, sgu_b, w_gate, w_branch, w_o, ctx_out):
    cc = split_cols(h_ctx @ w_in)
    cl = split_cols(h_lat @ w_in)
    qc, kc, vc, afc, abc = gla_inputs(cc[0], cc[1], cc[2], cc[4], cc[5], w_gk2, b_gk)
    ql, kl, vl, afl, abl = gla_inputs(cl[0], cl[1], cl[2], cl[4], cl[5], w_gk2, b_gk)
    bsz = h_ctx.shape[0]
    zero = jnp.zeros((bsz, GLA_HEADS, GLA_DK, GLA_DV), jnp.float32)
    if ctx_out:
        ocf, hcf = gla_direction(qc, kc, vc, afc, zero, False)
        ocb, hcb = gla_direction(qc, kc, vc, abc, zero, True)
    else:
        fl = lambda a: jnp.flip(a, 2)
        hcf = gla_final_state(kc, vc, afc)
        hcb = gla_final_state(fl(kc), fl(vc), fl(abc))
    olf, _ = gla_direction(ql, kl, vl, afl, hcf, False)
    olb, _ = gla_direction(ql, kl, vl, abl, hcb, True)
    y_lat = mixer_out(h_lat, cl, olf, olb, rows, gla_g, pool_w, pool_scale, sgu_ln_g, sgu_ln_b,
                      sgu_w, sgu_b, w_gate, w_branch, w_o)
    y_ctx = None
    if ctx_out:
        y_ctx = mixer_out(h_ctx, cc, ocf, ocb, None, gla_g, pool_w, pool_scale, sgu_ln_g, sgu_ln_b,
                          sgu_w, sgu_b, w_gate, w_branch, w_o)
    return y_ctx, y_lat


def swiglu(h, wg, wu, wd):
    return (jax.nn.silu(h @ wg) * (h @ wu)) @ wd


def moe_swiglu(h, w_router, wg, wu, wd):
    logits = (h @ w_router).astype(jnp.float32)
    top_v, top_i = lax.top_k(logits, TOP_K)
    top_w = jax.nn.softmax(top_v, axis=-1)
    gate = jnp.sum(jax.nn.one_hot(top_i, N_EXPERTS, dtype=jnp.float32) * top_w[..., None], axis=-2)
    gate = gate.astype(h.dtype)
    out = gate[..., 0:1] * swiglu(h, wg[0], wu[0], wd[0])
    for e in range(1, N_EXPERTS):
        out = out + gate[..., e:e + 1] * swiglu(h, wg[e], wu[e], wd[e])
    return out


def setup_inputs(seed: int = 0) -> dict:
    key = jax.random.key(seed)
    ks = iter(jax.random.split(key, 40))
    nrm = lambda shape, scale: jax.random.normal(next(ks), shape, jnp.float32) * scale
    D = D_MODEL
    n_dense = (DEPTH + 1) // 2
    n_moe = DEPTH // 2
    return {
        "x": nrm((BATCH, SEQ, D), 1.0),
        "c": nrm((BATCH, D), 1.0),
        "ctx": nrm((BATCH, CTX_LEN, D), 1.0),
        "c_ctx": nrm((D,), 1.0),
        "w_mod": nrm((DEPTH, D, 6 * D), 0.5 * D ** -0.5),
        "b_mod": nrm((DEPTH, 6 * D), 0.02),
        "w_in": nrm((DEPTH, D, N_IN), D ** -0.5),
        "w_gk2": nrm((DEPTH, 2, GK_RANK, QK_W), GK_RANK ** -0.5),
        "b_gk": 1.0 + nrm((DEPTH, 2, QK_W), 0.5),
        "gla_g": 1.0 + nrm((DEPTH, V_W), 0.02),
        "pool_w": nrm((DEPTH, len(POOL_WINDOWS), POOL_GROUP, POOL_GROUP), POOL_GROUP ** -0.5),
        "pool_scale": 1.0 + nrm((DEPTH, POOL_W), 0.02),
        "sgu_ln_g": 1.0 + nrm((DEPTH, SGU_W), 0.02),
        "sgu_ln_b": nrm((DEPTH, SGU_W), 0.02),
        "sgu_w": nrm((DEPTH, SGU_GROUPS, SGU_CHUNK, SGU_CHUNK), SGU_CHUNK ** -0.5),
        "sgu_b": 1.0 + nrm((DEPTH, SGU_GROUPS, SGU_CHUNK), 0.02),
        "w_gate": nrm((DEPTH, N_BRANCH, D, D), D ** -0.5),
        "w_branch": nrm((DEPTH, N_BRANCH, BRANCH_W, D), BRANCH_W ** -0.5),
        "w_o": nrm((DEPTH, D, D), BETA * D ** -0.5),
        "ln_g": 1.0 + nrm((DEPTH, 2, D), 0.02),
        "ln_b": nrm((DEPTH, 2, D), 0.02),
        "w_ffn_gate": nrm((n_dense, D, D_FF), D ** -0.5),
        "w_ffn_up": nrm((n_dense, D, D_FF), D ** -0.5),
        "w_ffn_down": nrm((n_dense, D_FF, D), BETA * D_FF ** -0.5),
        "w_router": nrm((n_moe, D, N_EXPERTS), D ** -0.5),
        "w_exp_gate": nrm((n_moe, N_EXPERTS, D, D_FF), D ** -0.5),
        "w_exp_up": nrm((n_moe, N_EXPERTS, D, D_FF), D ** -0.5),
        "w_exp_down": nrm((n_moe, N_EXPERTS, D_FF, D), BETA * D_FF ** -0.5),
    }


def reference(x, c, ctx, c_ctx, w_mod, b_mod, w_in, w_gk2, b_gk, gla_g, pool_w, pool_scale,
              sgu_ln_g, sgu_ln_b, sgu_w, sgu_b, w_gate, w_branch, w_o, ln_g, ln_b,
              w_ffn_gate, w_ffn_up, w_ffn_down, w_router, w_exp_gate, w_exp_up, w_exp_down):
    rows = x.shape[1] // GRID_W
    s_lat = jax.nn.silu(c)[:, None, :]
    s_ctx = jax.nn.silu(c_ctx)[None, None, :]
    xl, xc = x, ctx
    for l in range(DEPTH):
        last = l == DEPTH - 1
        ml = jnp.split(s_lat @ w_mod[l] + b_mod[l], 6, axis=-1)
        mc = jnp.split(s_ctx @ w_mod[l] + b_mod[l], 6, axis=-1)
        hl = modulate(xl, ml[0], ml[1])
        hc = modulate(xc, mc[0], mc[1])
        yc, yl = token_mix(hc, hl, rows, w_in[l], w_gk2[l], b_gk[l], gla_g[l], pool_w[l],
                           pool_scale[l], sgu_ln_g[l], sgu_ln_b[l], sgu_w[l], sgu_b[l],
                           w_gate[l], w_branch[l], w_o[l], not last)
        xl = layer_norm(ALPHA * xl + ml[2] * yl, ln_g[l, 0], ln_b[l, 0])
        if not last:
            xc = layer_norm(ALPHA * xc + mc[2] * yc, ln_g[l, 0], ln_b[l, 0])
        if l % 2 == 0:
            ffn = lambda h, i=l // 2: swiglu(h, w_ffn_gate[i], w_ffn_up[i], w_ffn_down[i])
        else:
            ffn = lambda h, i=l // 2: moe_swiglu(h, w_router[i], w_exp_gate[i], w_exp_up[i], w_exp_down[i])
        hl = modulate(xl, ml[3], ml[4])
        xl = layer_norm(ALPHA * xl + ml[5] * ffn(hl), ln_g[l, 1], ln_b[l, 1])
        if not last:
            hc = modulate(xc, mc[3], mc[4])
            xc = layer_norm(ALPHA * xc + mc[5] * ffn(hc), ln_g[l, 1], ln_b[l, 1])
    return xl
```

```python
import functools
import math

import jax
import jax.numpy as jnp
from jax import lax
from jax.experimental import pallas as pl
from jax.experimental.pallas import tpu as pltpu

F32 = jnp.float32
BF16 = jnp.bfloat16

GRID_W = 64
GLA_HEADS = 4
GK_TAU = 16.0
POOL_WINDOWS = (2, 4, 8, 16)
SGU_CHUNK = 128
TOP_K = 2
EPS = 1e-6

LANES = 128
SUBLANES = 8
VMEM_LIMIT_BYTES = 56 * 1024 * 1024

GLA_C = 128


def _cparams(*sem):
    return pltpu.CompilerParams(dimension_semantics=sem, vmem_limit_bytes=VMEM_LIMIT_BYTES)


def _resident(shape):
    nd = len(shape)
    return pl.BlockSpec(shape, lambda *_: (0,) * nd, pipeline_mode=pl.Buffered(1))


def _dot(a, b):
    return jnp.dot(a, b, preferred_element_type=F32)


def _dot_nt(a, b):
    return lax.dot_general(a, b, (((1,), (1,)), ((), ())), preferred_element_type=F32)


def _split3(x):
    hi = x.astype(BF16)
    r1 = x - hi.astype(F32)
    mid = r1.astype(BF16)
    lo = (r1 - mid.astype(F32)).astype(BF16)
    return hi, mid, lo


def _dot_exact_rhs(x, w):
    hi, mid, lo = _split3(x)
    return _dot(hi, w) + _dot(mid, w) + _dot(lo, w)


def _dot_exact_lhs(w, x):
    hi, mid, lo = _split3(x)
    return _dot(w, hi) + _dot(w, mid) + _dot(w, lo)


def _ln(x):
    mu = jnp.mean(x, axis=-1, keepdims=True)
    xc = x - mu
    var = jnp.mean(xc * xc, axis=-1, keepdims=True)
    return xc * lax.rsqrt(var + EPS)


def _sigmoid(x):
    return 1.0 / (1.0 + jnp.exp(-x))


def _silu(x):
    return x * _sigmoid(x)


def _gelu_tanh(x):
    return 0.5 * x * (1.0 + jnp.tanh(math.sqrt(2.0 / math.pi) * (x + 0.044715 * (x * x * x))))


def _pick_tile(n, candidates=(1024, 512, 256, 128)):
    for c in candidates:
        if n % c == 0:
            return c
    raise ValueError(f"no tile for {n}")


def _mod_kernel(c_ref, w_ref, b_ref, o_ref):
    s = _silu(c_ref[...])
    sh, sm, sl = _split3(s)
    w = w_ref[0]
    wh = w.astype(BF16)
    wl = (w - wh.astype(F32)).astype(BF16)
    acc = _dot(sh, wh) + (_dot(sh, wl) + _dot(sm, wh)) + (_dot(sm, wl) + _dot(sl, wh))
    o_ref[0] = acc + b_ref[0]


def _modulation(cvec, w_mod, b_mod):
    depth, d, n6 = w_mod.shape
    rows = cvec.shape[0]
    tn = _pick_tile(n6, (1536, 1024, 512, 256, 128))
    return pl.pallas_call(
        _mod_kernel,
        grid=(depth, n6 // tn),
        in_specs=[
            pl.BlockSpec((rows, d), lambda l, j: (0, 0)),
            pl.BlockSpec((1, d, tn), lambda l, j: (l, 0, j)),
            pl.BlockSpec((1, 1, tn), lambda l, j: (l, 0, j)),
        ],
        out_specs=pl.BlockSpec((1, rows, tn), lambda l, j: (l, 0, j)),
        out_shape=jax.ShapeDtypeStruct((depth, rows, n6), F32),
        compiler_params=_cparams("parallel", "parallel"),
    )(cvec, w_mod, b_mod.reshape(depth, 1, n6))


class _Stream:
    def __init__(self, n_tok, seq, mod_row_of_tile, tm):
        self.n_tok, self.seq, self.tm = n_tok, seq, tm
        self.mod_row_of_tile = mod_row_of_tile
        self.n_tiles = n_tok // tm

    def mod_spec(self, n6):
        f = self.mod_row_of_tile
        return pl.BlockSpec((1, 1, n6), lambda i, *_: (f(i), 0, 0))

    def tok_spec(self, width):
        return pl.BlockSpec((self.tm, width), lambda i, *_: (i, 0))


def _lat_stream(bsz, t):
    tm = _pick_tile(t, (512, 256, 128))
    per = t // tm
    return _Stream(bsz * t, t, lambda i: i // per, tm)


def _ctx_stream(bsz, tc):
    tm = _pick_tile(bsz * tc, (512, 256, 128))
    return _Stream(bsz * tc, tc, lambda i: bsz, tm)


def _inproj_kernel(x_ref, mod_ref, win_ref, wgk2_ref, bgk_ref,
                   h_ref, qkla_ref, v_ref, r_ref, pool_ref, fft_ref, sgu_ref, *, d, q_scale):
    x = x_ref[...]
    shift = mod_ref[0, :, 0:d]
    scale = mod_ref[0, :, d:2 * d]
    hb = (_ln(x) * (1.0 + scale) + shift).astype(BF16)
    h_ref[...] = hb
    p = _dot(hb, win_ref[...])
    qkla_ref[:, 0:128] = p[:, 0:128] * q_scale
    qkla_ref[:, 128:256] = p[:, 128:256]
    v_ref[...] = p[:, 256:512].astype(BF16)
    r_ref[...] = p[:, 512:768].astype(BF16)
    pool_ref[...] = p[:, 768:1024].astype(BF16)
    fft_ref[...] = p[:, 1024:1280].astype(BF16)
    sgu_ref[...] = p[:, 1280:1792].astype(BF16)
    z = _dot(p[:, 1792:1920].astype(BF16), wgk2_ref[...]) + bgk_ref[...]
    qkla_ref[:, 256:512] = (jnp.minimum(z, 0.0) - jnp.log(1.0 + jnp.exp(-jnp.abs(z)))) * (1.0 / GK_TAU)


def _inproj(st, x, mod3, win, wgk2, bgk):
    d = x.shape[1]
    n6 = mod3.shape[2]
    nin = win.shape[1]
    tm = st.tm
    outs = [
        jax.ShapeDtypeStruct((st.n_tok, d), BF16),
        jax.ShapeDtypeStruct((st.n_tok, 512), F32),
        jax.ShapeDtypeStruct((st.n_tok, 256), BF16),
        jax.ShapeDtypeStruct((st.n_tok, 256), BF16),
        jax.ShapeDtypeStruct((st.n_tok, 256), BF16),
        jax.ShapeDtypeStruct((st.n_tok, 256), BF16),
        jax.ShapeDtypeStruct((st.n_tok, 512), BF16),
    ]
    dk = 128 // GLA_HEADS
    return pl.pallas_call(
        functools.partial(_inproj_kernel, d=d, q_scale=dk ** -0.5),
        grid=(st.n_tiles,),
        in_specs=[st.tok_spec(d), st.mod_spec(n6), _resident((d, nin)), _resident((128, 256)),
                  _resident((1, 256))],
        out_specs=[st.tok_spec(o.shape[1]) for o in outs],
        out_shape=outs,
        compiler_params=_cparams("parallel"),
    )(x, mod3, win, wgk2, bgk)


def _gla_kernel(qk_l, v_l, vt_l, r_l, qk_c, v_c, vt_c, r_c, g_ref, y_l, y_c, ht, oacc_l, oacc_c,
                *, nch_l, nch_c):
    c = GLA_C
    nh = GLA_HEADS
    ii = lax.broadcasted_iota(jnp.int32, (c, c), 0)
    jj = lax.broadcasted_iota(jnp.int32, (c, c), 1)
    ii4 = lax.broadcasted_iota(jnp.int32, (nh * c, c), 0) & (c - 1)
    jj4 = lax.broadcasted_iota(jnp.int32, (nh * c, c), 1)
    lane_qk = lax.broadcasted_iota(jnp.int32, (c, 128), 1) // (128 // nh)
    lane_v = lax.broadcasted_iota(jnp.int32, (c, 256), 1) // (256 // nh)
    st_mask = (lax.broadcasted_iota(jnp.int32, (256, 128), 0) // (256 // nh)
               == lax.broadcasted_iota(jnp.int32, (256, 128), 1) // (128 // nh))

    def run(qk, v, vt, oacc, nch, fwd):
        tri = jnp.where((jj <= ii) if fwd else (jj >= ii), 1.0, 0.0).astype(BF16)
        pair4 = (jj4 <= ii4) if fwd else (jj4 > ii4)
        mid, last = (c // 2 - 1, c - 1) if fwd else (c // 2, 0)
        la0 = 256 if fwd else 384

        def step(n, carry):
            idx = n if fwd else nch - 1 - n
            rows = pl.ds(pl.multiple_of(idx * c, c), c)
            q = qk[rows, 0:128]
            k = qk[rows, 128:256]
            b = _dot_exact_lhs(tri, qk[rows, la0:la0 + 128])
            b_mid = b[mid:mid + 1, :]
            b_last = b[last:last + 1, :]
            qe = q * jnp.exp(b - b_mid)
            ke = (k * jnp.exp(b_mid - b)).astype(BF16)
            kt = (k * jnp.exp(b_last - b)).astype(BF16)
            qa = (qe * jnp.exp(b_mid)).astype(BF16)
            qs = jnp.concatenate([jnp.where(lane_qk == h, qe, 0.0) for h in range(nh)],
                                 axis=0).astype(BF16)
            att = jnp.where(pair4, _dot_nt(qs, ke), 0.0).astype(BF16)
            os_ = _dot(att, v[rows, :])
            o = _dot_nt(qa, ht[...].astype(BF16))
            for h in range(nh):
                o = o + jnp.where(lane_v == h, os_[h * c:(h + 1) * c, :], 0.0)
            if fwd:
                oacc[rows, :] = o
            else:
                oacc[rows, :] = oacc[rows, :] + o
            dht = _dot(vt[idx], kt)
            ht[...] = ht[...] * jnp.exp(b_last) + jnp.where(st_mask, dht, 0.0)
            return carry

        lax.fori_loop(0, nch, step, 0)

    for fwd in (True, False):
        ht[...] = jnp.zeros_like(ht)
        run(qk_c, v_c, vt_c, oacc_c, nch_c, fwd)
        run(qk_l, v_l, vt_l, oacc_l, nch_l, fwd)

    hv = 256 // nh
    mh = jnp.where(lax.broadcasted_iota(jnp.int32, (256, 256), 0) // hv
                   == lax.broadcasted_iota(jnp.int32, (256, 256), 1) // hv, 1.0 / hv, 0.0).astype(BF16)
    g = g_ref[...]

    def finish(oacc, r, y, nch):
        def step(n, carry):
            rows = pl.ds(pl.multiple_of(n * c, c), c)
            o = oacc[rows, :]
            sq = o * o
            sh = sq.astype(BF16)
            sl = (sq - sh.astype(F32)).astype(BF16)
            ms = _dot(sh, mh) + _dot(sl, mh)
            rr = r[rows, :].astype(F32)
            y[rows, :] = (o * lax.rsqrt(ms + EPS) * g * _silu(rr)).astype(BF16)
            return carry

        lax.fori_loop(0, nch, step, 0)

    finish(oacc_c, r_c, y_c, nch_c)
    finish(oacc_l, r_l, y_l, nch_l)


def _gla(bsz, t, tc, lat, ctx, gla_g):
    c = GLA_C
    nch_l, nch_c = t // c, tc // c

    def prep(arrs, seq):
        qkla, v, r = arrs
        nch = seq // c
        vt = v.reshape(bsz, nch, c, 256).transpose(0, 1, 3, 2)
        return (qkla.reshape(bsz, seq, 512), v.reshape(bsz, seq, 256), vt, r.reshape(bsz, seq, 256))

    def specs(seq):
        nch = seq // c
        return [pl.BlockSpec((None, seq, 512), lambda b: (b, 0, 0)),
                pl.BlockSpec((None, seq, 256), lambda b: (b, 0, 0)),
                pl.BlockSpec((None, nch, 256, c), lambda b: (b, 0, 0, 0)),
                pl.BlockSpec((None, seq, 256), lambda b: (b, 0, 0))]

    y_l, y_c = pl.pallas_call(
        functools.partial(_gla_kernel, nch_l=nch_l, nch_c=nch_c),
        grid=(bsz,),
        in_specs=specs(t) + specs(tc) + [_resident((1, 256))],
        out_specs=[pl.BlockSpec((None, t, 256), lambda b: (b, 0, 0)),
                   pl.BlockSpec((None, tc, 256), lambda b: (b, 0, 0))],
        out_shape=[jax.ShapeDtypeStruct((bsz, t, 256), BF16),
                   jax.ShapeDtypeStruct((bsz, tc, 256), BF16)],
        scratch_shapes=[pltpu.VMEM((256, 128), F32), pltpu.VMEM((t, 256), F32),
                        pltpu.VMEM((tc, 256), F32)],
        compiler_params=_cparams("parallel"),
    )(*prep(lat, t), *prep(ctx, tc), gla_g)
    return y_l.reshape(bsz * t, 256), y_c.reshape(bsz * tc, 256)


POOL_PACK = 4


def _pool_kernel(x_ref, s_ref, ic_ref, pw_ref, ps_ref, o_ref):
    x = x_ref[...]
    ic = ic_ref[...]
    m = _dot(s_ref[...], x) * jnp.concatenate([ic, ic], axis=-1)
    dlt = (m - x.astype(F32)).astype(BF16)
    o_ref[...] = (_dot(dlt, pw_ref[...]) * ps_ref[...]).astype(BF16)


def _window_bounds(n, win):
    pos = jnp.arange(n)
    lo = jnp.clip(pos - win // 2, 0, n)
    hi = jnp.clip(pos + (win - win // 2), 0, n)
    return lo, hi


def _pool_tables(seq, rows):
    mats, invs = [], []
    for win in POOL_WINDOWS:
        if rows is None:
            lo, hi = _window_bounds(seq, win)
            pos = jnp.arange(seq)
            member = (pos[None, :] >= lo[:, None]) & (pos[None, :] < hi[:, None])
            cnt = hi - lo
        else:
            tok = jnp.arange(seq)
            rr, cc = tok // GRID_W, tok % GRID_W
            lo_r, hi_r = _window_bounds(rows, win)
            lo_c, hi_c = _window_bounds(GRID_W, win)
            member = ((rr[None, :] >= lo_r[rr][:, None]) & (rr[None, :] < hi_r[rr][:, None])
                      & (cc[None, :] >= lo_c[cc][:, None]) & (cc[None, :] < hi_c[cc][:, None]))
            cnt = (hi_r - lo_r)[rr] * (hi_c - lo_c)[cc]
        mats.append(member.astype(BF16))
        invs.append(jnp.broadcast_to((1.0 / cnt.astype(F32))[:, None], (seq, LANES)))
    return jnp.stack(mats), jnp.stack(invs)


def _pool(bsz, seq, xp, tables, pw_bd, ps_t):
    smat, inv = tables
    ng = len(POOL_WINDOWS)
    grp = 256 // ng
    nq = bsz // POOL_PACK
    xq = xp.reshape(nq, POOL_PACK, seq, ng, grp).transpose(0, 3, 2, 1, 4).reshape(nq, ng, seq, 256)
    out = pl.pallas_call(
        _pool_kernel,
        grid=(ng, nq),
        in_specs=[pl.BlockSpec((None, None, seq, 256), lambda g, q: (q, g, 0, 0)),
                  pl.BlockSpec((None, seq, seq), lambda g, q: (g, 0, 0)),
                  pl.BlockSpec((None, seq, LANES), lambda g, q: (g, 0, 0)),
                  pl.BlockSpec((None, 256, 256), lambda g, q: (g, 0, 0)),
                  pl.BlockSpec((None, 1, 256), lambda g, q: (g, 0, 0))],
        out_specs=pl.BlockSpec((None, None, seq, 256), lambda g, q: (q, g, 0, 0)),
        out_shape=jax.ShapeDtypeStruct((nq, ng, seq, 256), BF16),
        compiler_params=_cparams("arbitrary", "arbitrary"),
    )(xq, smat, inv, pw_bd, ps_t)
    out = out.reshape(nq, ng, seq, POOL_PACK, grp).transpose(0, 3, 2, 1, 4)
    return out.reshape(bsz * seq, 256)


def _fourier_kernel(z_ref, cb_ref, sb_ref, ct_ref, st_ref, o_ref, *, scale):
    z = z_ref[...]
    zc = _dot(z, cb_ref[...]).astype(BF16)
    zs = _dot(z, sb_ref[...]).astype(BF16)
    y = _dot(ct_ref[...], zc) - _dot(st_ref[...], zs)
    o_ref[...] = (y * scale).astype(BF16)


def _dft_tables(n):
    k = jnp.arange(n, dtype=jnp.int32)
    ang = ((k[:, None] * k[None, :]) % n).astype(F32) * (2.0 * math.pi / n)
    return jnp.cos(ang), jnp.sin(ang)


def _fourier_chan_tables(ng):
    grp = 256 // ng
    cg, sg = _dft_tables(grp)
    eye = jnp.eye(ng, dtype=F32)
    return jnp.kron(eye, cg).astype(BF16), jnp.kron(eye, sg).astype(BF16)


def _fourier(bsz, seq, z, chan_tabs, seq_tabs):
    cb, sb = chan_tabs
    ct, st = seq_tabs
    grp = cb.shape[0] // 4
    out = pl.pallas_call(
        functools.partial(_fourier_kernel, scale=1.0 / math.sqrt(seq * grp)),
        grid=(bsz,),
        in_specs=[pl.BlockSpec((None, seq, 256), lambda b: (b, 0, 0)),
                  _resident((256, 256)), _resident((256, 256)),
                  _resident((seq, seq)), _resident((seq, seq))],
        out_specs=pl.BlockSpec((None, seq, 256), lambda b: (b, 0, 0)),
        out_shape=jax.ShapeDtypeStruct((bsz, seq, 256), BF16),
        compiler_params=_cparams("parallel"),
    )(z.reshape(bsz, seq, 256), cb, sb, ct, st)
    return out.reshape(bsz * seq, 256)


def _sgu_kernel(s_ref, lg_ref, lb_ref, w_ref, bm_ref, o_ref, *, n_chunks, ng):
    u = _gelu_tanh(s_ref[:, 0:256].astype(F32))
    v = _gelu_tanh(s_ref[:, 256:512].astype(F32))
    vb = (_ln(v) * lg_ref[...] + lb_ref[...]).astype(BF16)
    lane_g = lax.broadcasted_iota(jnp.int32, (SGU_CHUNK, 256), 1) // (256 // ng)
    for ch in range(n_chunks):
        rows = slice(ch * SGU_CHUNK, (ch + 1) * SGU_CHUNK)
        vc = vb[rows, :]
        s = bm_ref[...]
        for g in range(ng):
            s = s + jnp.where(lane_g == g, _dot(w_ref[g], vc), 0.0)
        o_ref[rows, :] = (u[rows, :] * s).astype(BF16)


def _sgu(st, sgu_in, lg, lb, w, bm):
    ng = w.shape[0]
    tm = st.tm
    return pl.pallas_call(
        functools.partial(_sgu_kernel, n_chunks=tm // SGU_CHUNK, ng=ng),
        grid=(st.n_tiles,),
        in_specs=[st.tok_spec(512), _resident((1, 256)), _resident((1, 256)),
                  _resident((ng, SGU_CHUNK, SGU_CHUNK)), _resident((SGU_CHUNK, 256))],
        out_specs=st.tok_spec(256),
        out_shape=jax.ShapeDtypeStruct((st.n_tok, 256), BF16),
        compiler_params=_cparams("parallel"),
    )(sgu_in, lg, lb, w, bm)


MERGE_COLS = 512


def _merge_kernel(x_ref, h_ref, y0, y1, y2, y3, mod_ref, wg_ref, wb_ref, wo_ref, lg_ref, lb_ref,
                  o_ref, *, d, alpha):
    h = h_ref[...]
    ys = (y0[...], y1[...], y2[...], y3[...])
    parts = []
    for cb in range(d // MERGE_COLS):
        cols = slice(cb * MERGE_COLS, (cb + 1) * MERGE_COLS)
        m = None
        for i in range(4):
            t = _sigmoid(_dot(h, wg_ref[i, :, cols])) * _dot(ys[i], wb_ref[i, :, cols])
            m = t if m is None else m + t
        parts.append(m.astype(BF16))
    y = _dot(jnp.concatenate(parts, axis=-1), wo_ref[...])
    gate = mod_ref[0, :, 2 * d:3 * d]
    xn = alpha * x_ref[...] + gate * y
    o_ref[...] = _ln(xn) * lg_ref[...] + lb_ref[...]


def _merge(st, x, h, ys, mod3, wg, wb, wo, lg, lb, alpha):
    d = x.shape[1]
    n6 = mod3.shape[2]
    return pl.pallas_call(
        functools.partial(_merge_kernel, d=d, alpha=alpha),
        grid=(st.n_tiles,),
        in_specs=[st.tok_spec(d), st.tok_spec(d)] + [st.tok_spec(256)] * 4
        + [st.mod_spec(n6), _resident(wg.shape), _resident(wb.shape), _resident(wo.shape),
           _resident((1, d)), _resident((1, d))],
        out_specs=st.tok_spec(d),
        out_shape=jax.ShapeDtypeStruct((st.n_tok, d), F32),
        compiler_params=_cparams("parallel"),
    )(x, h, *ys, mod3, wg, wb, wo, lg, lb)


def _ff_blocks(ff):
    if ff <= 1536 or ff % 256:
        return [(0, ff)]
    first = (ff // 512 + (1 if (ff // 256) % 2 else 0)) * 256
    return [(0, first), (first, ff)]


def _ffn_kernel(x_ref, mod_ref, wg_ref, wu_ref, wd_ref, lg_ref, lb_ref, o_ref, *, d, alpha, blocks):
    x = x_ref[...]
    shift = mod_ref[0, :, 3 * d:4 * d]
    scale = mod_ref[0, :, 4 * d:5 * d]
    gate = mod_ref[0, :, 5 * d:6 * d]
    hb = (_ln(x) * (1.0 + scale) + shift).astype(BF16)
    y = None
    for lo, hi in blocks:
        a = (_silu(_dot(hb, wg_ref[:, lo:hi])) * _dot(hb, wu_ref[:, lo:hi])).astype(BF16)
        t = _dot(a, wd_ref[lo:hi, :])
        y = t if y is None else y + t
    xn = alpha * x + gate * y
    o_ref[...] = _ln(xn) * lg_ref[...] + lb_ref[...]


def _ffn(st, x, mod3, wg, wu, wd, lg, lb, alpha):
    d = x.shape[1]
    n6 = mod3.shape[2]
    ff = wg.shape[1]
    return pl.pallas_call(
        functools.partial(_ffn_kernel, d=d, alpha=alpha, blocks=_ff_blocks(ff)),
        grid=(st.n_tiles,),
        in_specs=[st.tok_spec(d), st.mod_spec(n6), _resident(wg.shape), _resident(wu.shape),
                  _resident(wd.shape), _resident((1, d)), _resident((1, d))],
        out_specs=st.tok_spec(d),
        out_shape=jax.ShapeDtypeStruct((st.n_tok, d), F32),
        compiler_params=_cparams("parallel"),
    )(x, mod3, wg, wu, wd, lg, lb)


def _moe_kernel(x_ref, mod_ref, wr_ref, wg_ref, wu_ref, wd_ref, lg_ref, lb_ref, o_ref,
                hb_ref, gate_ref, acc_ref, *, d, alpha, n_exp, n_fb):
    e = pl.program_id(1)
    f = pl.program_id(2)

    @pl.when((e == 0) & (f == 0))
    def _():
        x = x_ref[...]
        shift = mod_ref[0, :, 3 * d:4 * d]
        scale = mod_ref[0, :, 4 * d:5 * d]
        h = _ln(x) * (1.0 + scale) + shift
        hb_ref[...] = h.astype(BF16)
        wr = wr_ref[...]
        wrh = wr.astype(BF16)
        wrl = (wr - wrh.astype(F32)).astype(BF16)
        hh, hm, hl = _split3(h)
        logits = _dot(hh, wrh) + (_dot(hh, wrl) + _dot(hm, wrh)) + (_dot(hm, wrl) + _dot(hl, wrh))
        lane = lax.broadcasted_iota(jnp.int32, logits.shape, 1)
        neg = jnp.float32(-jnp.inf)
        logits = jnp.where(lane < n_exp, logits, neg)
        lane_f = lane.astype(F32)
        v1 = jnp.max(logits, axis=-1, keepdims=True)
        i1 = jnp.min(jnp.where(logits == v1, lane_f, float(LANES)), axis=-1, keepdims=True)
        rest = jnp.where(lane_f == i1, neg, logits)
        v2 = jnp.max(rest, axis=-1, keepdims=True)
        i2 = jnp.min(jnp.where(rest == v2, lane_f, float(LANES)), axis=-1, keepdims=True)
        e2 = jnp.exp(v2 - v1)
        w1 = 1.0 / (1.0 + e2)
        w2 = e2 / (1.0 + e2)
        gate_ref[...] = jnp.where(lane_f == i1, w1, jnp.where(lane_f == i2, w2, 0.0))
        acc_ref[...] = jnp.zeros_like(acc_ref)

    hb = hb_ref[...]
    a = (_silu(_dot(hb, wg_ref[...])) * _dot(hb, wu_ref[...])).astype(BF16)
    t = _dot(a, wd_ref[...])
    gates = gate_ref[...]
    lane = lax.broadcasted_iota(jnp.int32, gates.shape, 1)
    ge = jnp.sum(jnp.where(lane == e, gates, 0.0), axis=-1, keepdims=True)
    acc_ref[...] += ge * t

    @pl.when((e == n_exp - 1) & (f == n_fb - 1))
    def _():
        gate = mod_ref[0, :, 5 * d:6 * d]
        xn = alpha * x_ref[...] + gate * acc_ref[...]
        o_ref[...] = _ln(xn) * lg_ref[...] + lb_ref[...]


def _moe(st, x, mod3, wr_pad, wg, wu, wd, lg, lb, alpha):
    d = x.shape[1]
    n6 = mod3.shape[2]
    n_exp, _, ff = wg.shape
    tm = st.tm
    fb = ff // 2 if (ff // 2) % LANES == 0 else ff
    n_fb = ff // fb
    return pl.pallas_call(
        functools.partial(_moe_kernel, d=d, alpha=alpha, n_exp=n_exp, n_fb=n_fb),
        grid=(st.n_tiles, n_exp, n_fb),
        in_specs=[pl.BlockSpec((tm, d), lambda i, e, f: (i, 0)),
                  st.mod_spec(n6),
                  _resident((d, LANES)),
                  pl.BlockSpec((None, d, fb), lambda i, e, f: (e, 0, f)),
                  pl.BlockSpec((None, d, fb), lambda i, e, f: (e, 0, f)),
                  pl.BlockSpec((None, fb, d), lambda i, e, f: (e, f, 0)),
                  _resident((1, d)), _resident((1, d))],
        out_specs=pl.BlockSpec((tm, d), lambda i, e, f: (i, 0)),
        out_shape=jax.ShapeDtypeStruct((st.n_tok, d), F32),
        scratch_shapes=[pltpu.VMEM((tm, d), BF16), pltpu.VMEM((tm, LANES), F32),
                        pltpu.VMEM((tm, d), F32)],
        compiler_params=_cparams("parallel", "arbitrary", "arbitrary"),
    )(x, mod3, wr_pad, wg, wu, wd, lg, lb)


def kernel(x, c, ctx, c_ctx, w_mod, b_mod, w_in, w_gk2, b_gk, gla_g, pool_w, pool_scale, sgu_ln_g, sgu_ln_b, sgu_w, sgu_b, w_gate, w_branch, w_o, ln_g, ln_b, w_ffn_gate, w_ffn_up, w_ffn_down, w_router, w_exp_gate, w_exp_up, w_exp_down):
    bsz, t, d = x.shape
    tc = ctx.shape[1]
    depth = w_mod.shape[0]
    n6 = w_mod.shape[2]
    alpha = (2 * depth) ** 0.25
    rows = t // GRID_W
    ng = len(POOL_WINDOWS)
    gk_rank = w_gk2.shape[2]
    assert d == 1024 and w_in.shape[2] == 1792 + 2 * gk_rank and 2 * gk_rank <= LANES
    assert t % GLA_C == 0 and tc % GLA_C == 0 and bsz % POOL_PACK == 0

    n_rows = -(-(bsz + 1) // SUBLANES) * SUBLANES
    cvec = jnp.zeros((n_rows, d), F32).at[:bsz].set(c).at[bsz].set(c_ctx)
    mod_all = _modulation(cvec, w_mod, b_mod)

    lat = _lat_stream(bsz, t)
    cst = _ctx_stream(bsz, tc)
    xl = x.reshape(bsz * t, d)
    xc = ctx.reshape(bsz * tc, d)

    pool_tab_l = _pool_tables(t, rows)
    pool_tab_c = _pool_tables(tc, None)
    chan_tabs = _fourier_chan_tables(ng)
    dft_l = tuple(a.astype(BF16) for a in _dft_tables(t))
    dft_c = tuple(a.astype(BF16) for a in _dft_tables(tc))

    gk0 = 768
    gk1 = gk0 + 2 * gk_rank

    for l in range(depth):
        last = l == depth - 1
        mod3 = mod_all[l].reshape(n_rows, 1, n6)
        win = jnp.concatenate(
            [w_in[l][:, :gk0], w_in[l][:, gk1:], w_in[l][:, gk0:gk1],
             jnp.zeros((d, LANES - 2 * gk_rank), F32)], axis=1).astype(BF16)
        wgk2 = jnp.zeros((LANES, 256), F32)
        wgk2 = wgk2.at[0:gk_rank, 0:128].set(w_gk2[l, 0]).at[gk_rank:2 * gk_rank, 128:256].set(w_gk2[l, 1])
        wgk2 = wgk2.astype(BF16)
        bgk = b_gk[l].reshape(1, 256)
        eye_p = jnp.eye(POOL_PACK, dtype=F32)
        pw_bd = jnp.stack([jnp.kron(eye_p, pool_w[l, g]) for g in range(ng)]).astype(BF16)
        ps_t = jnp.tile(pool_scale[l].reshape(ng, 1, 256 // ng), (1, 1, POOL_PACK))
        sgu_wb = sgu_w[l].astype(BF16)
        sgu_bm = jnp.repeat(sgu_b[l].T, 256 // ng, axis=1)
        wg = w_gate[l].astype(BF16)
        wb = w_branch[l].astype(BF16)
        wo = w_o[l].astype(BF16)
        lg0, lb0 = ln_g[l, 0].reshape(1, d), ln_b[l, 0].reshape(1, d)
        lg1, lb1 = ln_g[l, 1].reshape(1, d), ln_b[l, 1].reshape(1, d)

        hl, qk_l, v_l, r_l, p_l, f_l, s_l = _inproj(lat, xl, mod3, win, wgk2, bgk)
        hc, qk_c, v_c, r_c, p_c, f_c, s_c = _inproj(cst, xc, mod3, win, wgk2, bgk)
        ya_l, ya_c = _gla(bsz, t, tc, (qk_l, v_l, r_l), (qk_c, v_c, r_c), gla_g[l].reshape(1, 256))
        sg = (sgu_ln_g[l].reshape(1, 256), sgu_ln_b[l].reshape(1, 256), sgu_wb, sgu_bm)
        yb_l = _pool(bsz, t, p_l, pool_tab_l, pw_bd, ps_t)
        yc_l = _fourier(bsz, t, f_l, chan_tabs, dft_l)
        yd_l = _sgu(lat, s_l, *sg)
        xl = _merge(lat, xl, hl, (ya_l, yb_l, yc_l, yd_l), mod3, wg, wb, wo, lg0, lb0, alpha)
        if not last:
            yb_c = _pool(bsz, tc, p_c, pool_tab_c, pw_bd, ps_t)
            yc_c = _fourier(bsz, tc, f_c, chan_tabs, dft_c)
            yd_c = _sgu(cst, s_c, *sg)
            xc = _merge(cst, xc, hc, (ya_c, yb_c, yc_c, yd_c), mod3, wg, wb, wo, lg0, lb0, alpha)

        i = l // 2
        if l % 2 == 0:
            fw = (w_ffn_gate[i].astype(BF16), w_ffn_up[i].astype(BF16), w_ffn_down[i].astype(BF16))
            xl = _ffn(lat, xl, mod3, *fw, lg1, lb1, alpha)
            if not last:
                xc = _ffn(cst, xc, mod3, *fw, lg1, lb1, alpha)
        else:
            n_exp = w_router.shape[2]
            wr_pad = jnp.zeros((d, LANES), F32).at[:, :n_exp].set(w_router[i])
            ew = (w_exp_gate[i].astype(BF16), w_exp_up[i].astype(BF16), w_exp_down[i].astype(BF16))
            xl = _moe(lat, xl, mod3, wr_pad, *ew, lg1, lb1, alpha)
            if not last:
                xc = _moe(cst, xc, mod3, wr_pad, *ew, lg1, lb1, alpha)
    return xl.reshape(bsz, t, d)
```

```python
import functools
import math

import jax
import jax.numpy as jnp
from jax import lax
from jax.experimental import pallas as pl
from jax.experimental.pallas import tpu as pltpu

F32 = jnp.float32
BF16 = jnp.bfloat16

GRID_W = 64
GLA_HEADS = 4
GK_TAU = 16.0
POOL_WINDOWS = (2, 4, 8, 16)
SGU_CHUNK = 128
TOP_K = 2
EPS = 1e-6

LANES = 128
SUBLANES = 8
VMEM_LIMIT_BYTES = 56 * 1024 * 1024

GLA_C = 128


def _cparams(*sem):
    return pltpu.CompilerParams(dimension_semantics=sem, vmem_limit_bytes=VMEM_LIMIT_BYTES)


def _resident(shape):
    nd = len(shape)
    return pl.BlockSpec(shape, lambda *_: (0,) * nd, pipeline_mode=pl.Buffered(1))


def _dot(a, b):
    return jnp.dot(a, b, preferred_element_type=F32)


def _dot_nt(a, b):
    return lax.dot_general(a, b, (((1,), (1,)), ((), ())), preferred_element_type=F32)


def _split3(x):
    hi = x.astype(BF16)
    r1 = x - hi.astype(F32)
    mid = r1.astype(BF16)
    lo = (r1 - mid.astype(F32)).astype(BF16)
    return hi, mid, lo


def _dot_exact_rhs(x, w):
    hi, mid, lo = _split3(x)
    return _dot(hi, w) + _dot(mid, w) + _dot(lo, w)


def _dot_exact_lhs(w, x):
    hi, mid, lo = _split3(x)
    return _dot(w, hi) + _dot(w, mid) + _dot(w, lo)


def _ln(x):
    mu = jnp.mean(x, axis=-1, keepdims=True)
    xc = x - mu
    var = jnp.mean(xc * xc, axis=-1, keepdims=True)
    return xc * lax.rsqrt(var + EPS)


def _sigmoid(x):
    return 1.0 / (1.0 + jnp.exp(-x))


def _silu(x):
    return x * _sigmoid(x)


def _gelu_tanh(x):
    return 0.5 * x * (1.0 + jnp.tanh(math.sqrt(2.0 / math.pi) * (x + 0.044715 * (x * x * x))))


def _pick_tile(n, candidates=(1024, 512, 256, 128)):
    for c in candidates:
        if n % c == 0:
            return c
    raise ValueError(f"no tile for {n}")


def _mod_kernel(c_ref, w_ref, b_ref, o_ref):
    s = _silu(c_ref[...])
    sh, sm, sl = _split3(s)
    w = w_ref[0]
    wh = w.astype(BF16)
    wl = (w - wh.astype(F32)).astype(BF16)
    acc = _dot(sh, wh) + (_dot(sh, wl) + _dot(sm, wh)) + (_dot(sm, wl) + _dot(sl, wh))
    o_ref[0] = acc + b_ref[0]


def _modulation(cvec, w_mod, b_mod):
    depth, d, n6 = w_mod.shape
    rows = cvec.shape[0]
    tn = _pick_tile(n6, (1536, 1024, 512, 256, 128))
    return pl.pallas_call(
        _mod_kernel,
        grid=(depth, n6 // tn),
        in_specs=[
            pl.BlockSpec((rows, d), lambda l, j: (0, 0)),
            pl.BlockSpec((1, d, tn), lambda l, j: (l, 0, j)),
            pl.BlockSpec((1, 1, tn), lambda l, j: (l, 0, j)),
        ],
        out_specs=pl.BlockSpec((1, rows, tn), lambda l, j: (l, 0, j)),
        out_shape=jax.ShapeDtypeStruct((depth, rows, n6), F32),
        compiler_params=_cparams("parallel", "parallel"),
        name="modulation",
    )(cvec, w_mod, b_mod.reshape(depth, 1, n6))


class _Stream:
    def __init__(self, n_tok, seq, mod_row_of_tile, tm):
        self.n_tok, self.seq, self.tm = n_tok, seq, tm
        self.mod_row_of_tile = mod_row_of_tile
        self.n_tiles = n_tok // tm

    def mod_spec(self, n6):
        f = self.mod_row_of_tile
        return pl.BlockSpec((1, 1, n6), lambda i, *_: (f(i), 0, 0))

    def tok_spec(self, width):
        return pl.BlockSpec((self.tm, width), lambda i, *_: (i, 0))


def _lat_stream(bsz, t):
    tm = _pick_tile(t, (512, 256, 128))
    per = t // tm
    return _Stream(bsz * t, t, lambda i: i // per, tm)


def _ctx_stream(bsz, tc):
    tm = _pick_tile(bsz * tc, (512, 256, 128))
    return _Stream(bsz * tc, tc, lambda i: bsz, tm)


def _inproj_kernel(x_ref, mod_ref, win_ref, wgk2_ref, bgk_ref,
                   h_ref, qkla_ref, v_ref, r_ref, pool_ref, fft_ref, sgu_ref, *, d, q_scale):
    x = x_ref[...]
    shift = mod_ref[0, :, 0:d]
    scale = mod_ref[0, :, d:2 * d]
    hb = (_ln(x) * (1.0 + scale) + shift).astype(BF16)
    h_ref[...] = hb
    p = _dot(hb, win_ref[...])
    qkla_ref[:, 0:128] = p[:, 0:128] * q_scale
    qkla_ref[:, 128:256] = p[:, 128:256]
    v_ref[...] = p[:, 256:512].astype(BF16)
    r_ref[...] = p[:, 512:768].astype(BF16)
    pool_ref[...] = p[:, 768:1024].astype(BF16)
    fft_ref[...] = p[:, 1024:1280].astype(BF16)
    sgu_ref[...] = p[:, 1280:1792].astype(BF16)
    z = _dot(p[:, 1792:1920].astype(BF16), wgk2_ref[...]) + bgk_ref[...]
    qkla_ref[:, 256:512] = (jnp.minimum(z, 0.0) - jnp.log(1.0 + jnp.exp(-jnp.abs(z)))) * (1.0 / GK_TAU)


def _inproj(st, x, mod3, win, wgk2, bgk):
    d = x.shape[1]
    n6 = mod3.shape[2]
    nin = win.shape[1]
    tm = st.tm
    outs = [
        jax.ShapeDtypeStruct((st.n_tok, d), BF16),
        jax.ShapeDtypeStruct((st.n_tok, 512), F32),
        jax.ShapeDtypeStruct((st.n_tok, 256), BF16),
        jax.ShapeDtypeStruct((st.n_tok, 256), BF16),
        jax.ShapeDtypeStruct((st.n_tok, 256), BF16),
        jax.ShapeDtypeStruct((st.n_tok, 256), BF16),
        jax.ShapeDtypeStruct((st.n_tok, 512), BF16),
    ]
    dk = 128 // GLA_HEADS
    return pl.pallas_call(
        functools.partial(_inproj_kernel, d=d, q_scale=dk ** -0.5),
        grid=(st.n_tiles,),
        in_specs=[st.tok_spec(d), st.mod_spec(n6), _resident((d, nin)), _resident((128, 256)),
                  _resident((1, 256))],
        out_specs=[st.tok_spec(o.shape[1]) for o in outs],
        out_shape=outs,
        compiler_params=_cparams("parallel"),
        name="in_proj",
    )(x, mod3, win, wgk2, bgk)


def _gla_kernel(qk_l, v_l, vt_l, r_l, qk_c, v_c, vt_c, r_c, g_ref, y_l, y_c, ht, oacc_l, oacc_c,
                *, nch_l, nch_c):
    c = GLA_C
    nh = GLA_HEADS
    ii = lax.broadcasted_iota(jnp.int32, (c, c), 0)
    jj = lax.broadcasted_iota(jnp.int32, (c, c), 1)
    ii4 = lax.broadcasted_iota(jnp.int32, (nh * c, c), 0) & (c - 1)
    jj4 = lax.broadcasted_iota(jnp.int32, (nh * c, c), 1)
    lane_qk = lax.broadcasted_iota(jnp.int32, (c, 128), 1) // (128 // nh)
    lane_v = lax.broadcasted_iota(jnp.int32, (c, 256), 1) // (256 // nh)
    st_mask = (lax.broadcasted_iota(jnp.int32, (256, 128), 0) // (256 // nh)
               == lax.broadcasted_iota(jnp.int32, (256, 128), 1) // (128 // nh))

    def run(qk, v, vt, oacc, nch, fwd):
        tri = jnp.where((jj <= ii) if fwd else (jj >= ii), 1.0, 0.0).astype(BF16)
        pair4 = (jj4 <= ii4) if fwd else (jj4 > ii4)
        mid, last = (c // 2 - 1, c - 1) if fwd else (c // 2, 0)
        la0 = 256 if fwd else 384

        def step(n, carry):
            idx = n if fwd else nch - 1 - n
            rows = pl.ds(pl.multiple_of(idx * c, c), c)
            q = qk[rows, 0:128]
            k = qk[rows, 128:256]
            b = _dot_exact_lhs(tri, qk[rows, la0:la0 + 128])
            b_mid = b[mid:mid + 1, :]
            b_last = b[last:last + 1, :]
            qe = q * jnp.exp(b - b_mid)
            ke = (k * jnp.exp(b_mid - b)).astype(BF16)
            kt = (k * jnp.exp(b_last - b)).astype(BF16)
            qa = (qe * jnp.exp(b_mid)).astype(BF16)
            qs = jnp.concatenate([jnp.where(lane_qk == h, qe, 0.0) for h in range(nh)],
                                 axis=0).astype(BF16)
            att = jnp.where(pair4, _dot_nt(qs, ke), 0.0).astype(BF16)
            os_ = _dot(att, v[rows, :])
            o = _dot_nt(qa, ht[...].astype(BF16))
            for h in range(nh):
                o = o + jnp.where(lane_v == h, os_[h * c:(h + 1) * c, :], 0.0)
            if fwd:
                oacc[rows, :] = o
            else:
                oacc[rows, :] = oacc[rows, :] + o
            dht = _dot(vt[idx], kt)
            ht[...] = ht[...] * jnp.exp(b_last) + jnp.where(st_mask, dht, 0.0)
            return carry

        lax.fori_loop(0, nch, step, 0)

    for fwd in (True, False):
        ht[...] = jnp.zeros_like(ht)
        run(qk_c, v_c, vt_c, oacc_c, nch_c, fwd)
        run(qk_l, v_l, vt_l, oacc_l, nch_l, fwd)

    hv = 256 // nh
    mh = jnp.where(lax.broadcasted_iota(jnp.int32, (256, 256), 0) // hv
                   == lax.broadcasted_iota(jnp.int32, (256, 256), 1) // hv, 1.0 / hv, 0.0).astype(BF16)
    g = g_ref[...]

    def finish(oacc, r, y, nch):
        def step(n, carry):
            rows = pl.ds(pl.multiple_of(n * c, c), c)
            o = oacc[rows, :]
            sq = o * o
            sh = sq.astype(BF16)
            sl = (sq - sh.astype(F32)).astype(BF16)
            ms = _dot(sh, mh) + _dot(sl, mh)
            rr = r[rows, :].astype(F32)
            y[rows, :] = (o * lax.rsqrt(ms + EPS) * g * _silu(rr)).astype(BF16)
            return carry

        lax.fori_loop(0, nch, step, 0)

    finish(oacc_c, r_c, y_c, nch_c)
    finish(oacc_l, r_l, y_l, nch_l)


def _gla(bsz, t, tc, lat, ctx, gla_g):
    c = GLA_C
    nch_l, nch_c = t // c, tc // c

    def prep(arrs, seq):
        qkla, v, r = arrs
        nch = seq // c
        vt = v.reshape(bsz, nch, c, 256).transpose(0, 1, 3, 2)
        return (qkla.reshape(bsz, seq, 512), v.reshape(bsz, seq, 256), vt, r.reshape(bsz, seq, 256))

    def specs(seq):
        nch = seq // c
        return [pl.BlockSpec((None, seq, 512), lambda b: (b, 0, 0)),
                pl.BlockSpec((None, seq, 256), lambda b: (b, 0, 0)),
                pl.BlockSpec((None, nch, 256, c), lambda b: (b, 0, 0, 0)),
                pl.BlockSpec((None, seq, 256), lambda b: (b, 0, 0))]

    y_l, y_c = pl.pallas_call(
        functools.partial(_gla_kernel, nch_l=nch_l, nch_c=nch_c),
        grid=(bsz,),
        in_specs=specs(t) + specs(tc) + [_resident((1, 256))],
        out_specs=[pl.BlockSpec((None, t, 256), lambda b: (b, 0, 0)),
                   pl.BlockSpec((None, tc, 256), lambda b: (b, 0, 0))],
        out_shape=[jax.ShapeDtypeStruct((bsz, t, 256), BF16),
                   jax.ShapeDtypeStruct((bsz, tc, 256), BF16)],
        scratch_shapes=[pltpu.VMEM((256, 128), F32), pltpu.VMEM((t, 256), F32),
                        pltpu.VMEM((tc, 256), F32)],
        compiler_params=_cparams("parallel"),
        name="gla",
    )(*prep(lat, t), *prep(ctx, tc), gla_g)
    return y_l.reshape(bsz * t, 256), y_c.reshape(bsz * tc, 256)


POOL_PACK = 4


def _pool_kernel(x_ref, s_ref, ic_ref, pw_ref, ps_ref, o_ref):
    x = x_ref[...]
    ic = ic_ref[...]
    m = _dot(s_ref[...], x) * jnp.concatenate([ic, ic], axis=-1)
    dlt = (m - x.astype(F32)).astype(BF16)
    o_ref[...] = (_dot(dlt, pw_ref[...]) * ps_ref[...]).astype(BF16)


def _window_bounds(n, win):
    pos = jnp.arange(n)
    lo = jnp.clip(pos - win // 2, 0, n)
    hi = jnp.clip(pos + (win - win // 2), 0, n)
    return lo, hi


def _pool_tables(seq, rows):
    mats, invs = [], []
    for win in POOL_WINDOWS:
        if rows is None:
            lo, hi = _window_bounds(seq, win)
            pos = jnp.arange(seq)
            member = (pos[None, :] >= lo[:, None]) & (pos[None, :] < hi[:, None])
            cnt = hi - lo
        else:
            tok = jnp.arange(seq)
            rr, cc = tok // GRID_W, tok % GRID_W
            lo_r, hi_r = _window_bounds(rows, win)
            lo_c, hi_c = _window_bounds(GRID_W, win)
            member = ((rr[None, :] >= lo_r[rr][:, None]) & (rr[None, :] < hi_r[rr][:, None])
                      & (cc[None, :] >= lo_c[cc][:, None]) & (cc[None, :] < hi_c[cc][:, None]))
            cnt = (hi_r - lo_r)[rr] * (hi_c - lo_c)[cc]
        mats.append(member.astype(BF16))
        invs.append(jnp.broadcast_to((1.0 / cnt.astype(F32))[:, None], (seq, LANES)))
    return jnp.stack(mats), jnp.stack(invs)


def _pool(bsz, seq, xp, tables, pw_bd, ps_t):
    smat, inv = tables
    ng = len(POOL_WINDOWS)
    grp = 256 // ng
    nq = bsz // POOL_PACK
    xq = xp.reshape(nq, POOL_PACK, seq, ng, grp).transpose(0, 3, 2, 1, 4).reshape(nq, ng, seq, 256)
    out = pl.pallas_call(
        _pool_kernel,
        grid=(ng, nq),
        in_specs=[pl.BlockSpec((None, None, seq, 256), lambda g, q: (q, g, 0, 0)),
                  pl.BlockSpec((None, seq, seq), lambda g, q: (g, 0, 0)),
                  pl.BlockSpec((None, seq, LANES), lambda g, q: (g, 0, 0)),
                  pl.BlockSpec((None, 256, 256), lambda g, q: (g, 0, 0)),
                  pl.BlockSpec((None, 1, 256), lambda g, q: (g, 0, 0))],
        out_specs=pl.BlockSpec((None, None, seq, 256), lambda g, q: (q, g, 0, 0)),
        out_shape=jax.ShapeDtypeStruct((nq, ng, seq, 256), BF16),
        compiler_params=_cparams("arbitrary", "arbitrary"),
        name="pool",
    )(xq, smat, inv, pw_bd, ps_t)
    out = out.reshape(nq, ng, seq, POOL_PACK, grp).transpose(0, 3, 2, 1, 4)
    return out.reshape(bsz * seq, 256)


def _fourier_kernel(z_ref, cb_ref, sb_ref, ct_ref, st_ref, o_ref, *, scale):
    z = z_ref[...]
    zc = _dot(z, cb_ref[...]).astype(BF16)
    zs = _dot(z, sb_ref[...]).astype(BF16)
    y = _dot(ct_ref[...], zc) - _dot(st_ref[...], zs)
    o_ref[...] = (y * scale).astype(BF16)


def _dft_tables(n):
    k = jnp.arange(n, dtype=jnp.int32)
    ang = ((k[:, None] * k[None, :]) % n).astype(F32) * (2.0 * math.pi / n)
    return jnp.cos(ang), jnp.sin(ang)


def _fourier_chan_tables(ng):
    grp = 256 // ng
    cg, sg = _dft_tables(grp)
    eye = jnp.eye(ng, dtype=F32)
    return jnp.kron(eye, cg).astype(BF16), jnp.kron(eye, sg).astype(BF16)


def _fourier(bsz, seq, z, chan_tabs, seq_tabs):
    cb, sb = chan_tabs
    ct, st = seq_tabs
    grp = cb.shape[0] // 4
    out = pl.pallas_call(
        functools.partial(_fourier_kernel, scale=1.0 / math.sqrt(seq * grp)),
        grid=(bsz,),
        in_specs=[pl.BlockSpec((None, seq, 256), lambda b: (b, 0, 0)),
                  _resident((256, 256)), _resident((256, 256)),
                  _resident((seq, seq)), _resident((seq, seq))],
        out_specs=pl.BlockSpec((None, seq, 256), lambda b: (b, 0, 0)),
        out_shape=jax.ShapeDtypeStruct((bsz, seq, 256), BF16),
        compiler_params=_cparams("parallel"),
        name="fourier",
    )(z.reshape(bsz, seq, 256), cb, sb, ct, st)
    return out.reshape(bsz * seq, 256)


def _sgu_kernel(s_ref, lg_ref, lb_ref, w_ref, bm_ref, o_ref, *, n_chunks, ng):
    u = _gelu_tanh(s_ref[:, 0:256].astype(F32))
    v = _gelu_tanh(s_ref[:, 256:512].astype(F32))
    vb = (_ln(v) * lg_ref[...] + lb_ref[...]).astype(BF16)
    lane_g = lax.broadcasted_iota(jnp.int32, (SGU_CHUNK, 256), 1) // (256 // ng)
    for ch in range(n_chunks):
        rows = slice(ch * SGU_CHUNK, (ch + 1) * SGU_CHUNK)
        vc = vb[rows, :]
        s = bm_ref[...]
        for g in range(ng):
            s = s + jnp.where(lane_g == g, _dot(w_ref[g], vc), 0.0)
        o_ref[rows, :] = (u[rows, :] * s).astype(BF16)


def _sgu(st, sgu_in, lg, lb, w, bm):
    ng = w.shape[0]
    tm = st.tm
    return pl.pallas_call(
        functools.partial(_sgu_kernel, n_chunks=tm // SGU_CHUNK, ng=ng),
        grid=(st.n_tiles,),
        in_specs=[st.tok_spec(512), _resident((1, 256)), _resident((1, 256)),
                  _resident((ng, SGU_CHUNK, SGU_CHUNK)), _resident((SGU_CHUNK, 256))],
        out_specs=st.tok_spec(256),
        out_shape=jax.ShapeDtypeStruct((st.n_tok, 256), BF16),
        compiler_params=_cparams("parallel"),
        name="sgu",
    )(sgu_in, lg, lb, w, bm)


MERGE_COLS = 512


def _merge_kernel(x_ref, h_ref, y0, y1, y2, y3, mod_ref, wg_ref, wb_ref, wo_ref, lg_ref, lb_ref,
                  o_ref, *, d, alpha):
    h = h_ref[...]
    ys = (y0[...], y1[...], y2[...], y3[...])
    parts = []
    for cb in range(d // MERGE_COLS):
        cols = slice(cb * MERGE_COLS, (cb + 1) * MERGE_COLS)
        m = None
        for i in range(4):
            t = _sigmoid(_dot(h, wg_ref[i, :, cols])) * _dot(ys[i], wb_ref[i, :, cols])
            m = t if m is None else m + t
        parts.append(m.astype(BF16))
    y = _dot(jnp.concatenate(parts, axis=-1), wo_ref[...])
    gate = mod_ref[0, :, 2 * d:3 * d]
    xn = alpha * x_ref[...] + gate * y
    o_ref[...] = _ln(xn) * lg_ref[...] + lb_ref[...]


def _merge(st, x, h, ys, mod3, wg, wb, wo, lg, lb, alpha):
    d = x.shape[1]
    n6 = mod3.shape[2]
    return pl.pallas_call(
        functools.partial(_merge_kernel, d=d, alpha=alpha),
        grid=(st.n_tiles,),
        in_specs=[st.tok_spec(d), st.tok_spec(d)] + [st.tok_spec(256)] * 4
        + [st.mod_spec(n6), _resident(wg.shape), _resident(wb.shape), _resident(wo.shape),
           _resident((1, d)), _resident((1, d))],
        out_specs=st.tok_spec(d),
        out_shape=jax.ShapeDtypeStruct((st.n_tok, d), F32),
        compiler_params=_cparams("parallel"),
        name="merge",
    )(x, h, *ys, mod3, wg, wb, wo, lg, lb)


def _ff_blocks(ff):
    if ff <= 1536 or ff % 256:
        return [(0, ff)]
    first = (ff // 512 + (1 if (ff // 256) % 2 else 0)) * 256
    return [(0, first), (first, ff)]


def _ffn_kernel(x_ref, mod_ref, wg_ref, wu_ref, wd_ref, lg_ref, lb_ref, o_ref, *, d, alpha, blocks):
    x = x_ref[...]
    shift = mod_ref[0, :, 3 * d:4 * d]
    scale = mod_ref[0, :, 4 * d:5 * d]
    gate = mod_ref[0, :, 5 * d:6 * d]
    hb = (_ln(x) * (1.0 + scale) + shift).astype(BF16)
    y = None
    for lo, hi in blocks:
        a = (_silu(_dot(hb, wg_ref[:, lo:hi])) * _dot(hb, wu_ref[:, lo:hi])).astype(BF16)
        t = _dot(a, wd_ref[lo:hi, :])
        y = t if y is None else y + t
    xn = alpha * x + gate * y
    o_ref[...] = _ln(xn) * lg_ref[...] + lb_ref[...]


def _ffn(st, x, mod3, wg, wu, wd, lg, lb, alpha):
    d = x.shape[1]
    n6 = mod3.shape[2]
    ff = wg.shape[1]
    return pl.pallas_call(
        functools.partial(_ffn_kernel, d=d, alpha=alpha, blocks=_ff_blocks(ff)),
        grid=(st.n_tiles,),
        in_specs=[st.tok_spec(d), st.mod_spec(n6), _resident(wg.shape), _resident(wu.shape),
                  _resident(wd.shape), _resident((1, d)), _resident((1, d))],
        out_specs=st.tok_spec(d),
        out_shape=jax.ShapeDtypeStruct((st.n_tok, d), F32),
        compiler_params=_cparams("parallel"),
        name="ffn",
    )(x, mod3, wg, wu, wd, lg, lb)


MOE_PIECE = 16
MOE_ROW_TILE = 512
MOE_META = LANES


def _moe_local_rows(tm, n_exp):
    return -(-(TOP_K * tm + n_exp * (MOE_PIECE - 1)) // LANES) * LANES


ROUTE_ROWS = SUBLANES


def _router_kernel(x_ref, mod_ref, wr_ref, hb_ref, rt_ref, cnt_ref, *, d, n_exp, tm):
    i = pl.program_id(0)
    x = x_ref[...]
    shift = mod_ref[0, :, 3 * d:4 * d]
    scale = mod_ref[0, :, 4 * d:5 * d]
    h = _ln(x) * (1.0 + scale) + shift
    hb = h.astype(BF16)
    hb_ref[...] = hb

    wr = wr_ref[...]
    wrh = wr.astype(BF16)
    wrl = (wr - wrh.astype(F32)).astype(BF16)
    hl = (h - hb.astype(F32)).astype(BF16)
    logits = _dot(hb, wrh) + (_dot(hb, wrl) + _dot(hl, wrh))
    lt = logits.T[0:n_exp, :]
    sub = lax.broadcasted_iota(jnp.int32, (n_exp, tm), 0).astype(F32)
    neg = jnp.float32(-jnp.inf)
    v1 = jnp.max(lt, axis=0, keepdims=True)
    i1 = jnp.min(jnp.where(lt == v1, sub, float(n_exp)), axis=0, keepdims=True)
    rest = jnp.where(sub == i1, neg, lt)
    v2 = jnp.max(rest, axis=0, keepdims=True)
    i2 = jnp.min(jnp.where(rest == v2, sub, float(n_exp)), axis=0, keepdims=True)
    e2 = jnp.exp(v2 - v1)
    w1 = 1.0 / (1.0 + e2)
    w2 = e2 / (1.0 + e2)
    oht = jnp.where(sub == i1, 1.0, jnp.where(sub == i2, 1.0, 0.0))
    cnt_col = jnp.sum(oht, axis=1, keepdims=True)
    for e in range(n_exp):
        cnt_ref[i, e] = cnt_col[e, 0].astype(jnp.int32)
    rec = lax.broadcasted_iota(jnp.int32, (ROUTE_ROWS, tm), 0)
    rt_ref[...] = jnp.where(rec == 0, i1, jnp.where(rec == 1, i2, jnp.where(
        rec == 2, w1, jnp.where(rec == 3, w2, 0.0))))


def _dispatch_kernel(seg_ref, zf_ref, hb_ref, rt_ref, u_ref, xs_hbm, meta_ref,
                     sorted_ref, zero_ref, ztile_ref, sem, *, d, n_exp, tm, ls, n_tiles, tf):
    i = pl.program_id(0)
    hb = hb_ref[...]
    i1 = rt_ref[0:1, :]
    i2 = rt_ref[1:2, :]
    w1 = rt_ref[2:3, :]
    w2 = rt_ref[3:4, :]
    sub = lax.broadcasted_iota(jnp.int32, (n_exp, tm), 0).astype(F32)
    oh1 = sub == i1
    oh2 = sub == i2
    oht = jnp.where(oh1, 1.0, jnp.where(oh2, 1.0, 0.0))
    cumt = _dot(oht.astype(BF16), u_ref[...])
    cnts = [seg_ref[i, n_exp + e] for e in range(n_exp)]
    npc = [(cn + (MOE_PIECE - 1)) // MOE_PIECE for cn in cnts]
    los = []
    lo = jnp.int32(0)
    for e in range(n_exp):
        los.append(lo)
        lo = lo + npc[e] * MOE_PIECE
    lo_b = jnp.zeros((n_exp, tm), F32)
    for e in range(n_exp):
        lo_b = jnp.where(sub == float(e), los[e].astype(F32), lo_b)
    lpt = cumt + lo_b
    lp1 = jnp.sum(jnp.where(oh1, lpt, 0.0), axis=0, keepdims=True)
    lp2 = jnp.sum(jnp.where(oh2, lpt, 0.0), axis=0, keepdims=True)

    row = lax.broadcasted_iota(jnp.int32, (ls, tm), 0).astype(F32)
    m1 = row == lp1
    m2 = row == lp2
    perm = jnp.where(m1, 1.0, jnp.where(m2, 1.0, 0.0)).astype(BF16)
    gcol = jnp.sum(jnp.where(m1, w1, jnp.where(m2, w2, 0.0)), axis=1, keepdims=True)
    sorted_ref[:, 0:d] = _dot(perm, hb).astype(BF16)
    ghi = gcol.astype(BF16).astype(F32)
    r1 = gcol - ghi
    gmid = r1.astype(BF16).astype(F32)
    glo = r1 - gmid
    lane = lax.broadcasted_iota(jnp.int32, (ls, MOE_META), 1)
    sorted_ref[:, d:d + MOE_META] = jnp.where(
        lane == 0, ghi, jnp.where(lane == 1, gmid, jnp.where(lane == 2, glo, 0.0))).astype(BF16)

    rows8 = jnp.where(sub == 0.0, lp1, jnp.where(sub == 1.0, lp2, 0.0))
    padded = jnp.concatenate([rows8, jnp.zeros((LANES - n_exp, tm), F32)], axis=0)
    meta_ref[...] = padded.T

    def piece(src_ref, src_row, dst_row):
        return pltpu.make_async_copy(
            src_ref.at[pl.ds(pl.multiple_of(src_row, MOE_PIECE), MOE_PIECE), :],
            xs_hbm.at[pl.ds(pl.multiple_of(dst_row, MOE_PIECE), MOE_PIECE), :], sem)

    total = jnp.int32(0)
    for e in range(n_exp):
        base = seg_ref[i, e]

        def start(j, carry, e=e, base=base):
            piece(sorted_ref, los[e] + j * MOE_PIECE, base + j * MOE_PIECE).start()
            return carry

        lax.fori_loop(0, npc[e], start, 0)
        total = total + npc[e]

    def wait(j, carry):
        piece(sorted_ref, 0, 0).wait()
        return carry

    lax.fori_loop(0, total, wait, 0)

    @pl.when(i == n_tiles - 1)
    def _():
        zero_ref[...] = jnp.zeros_like(zero_ref)
        ztile_ref[...] = jnp.zeros_like(ztile_ref)
        n_zero = jnp.int32(0)
        for e in range(n_exp):
            z0 = zf_ref[2 * e]
            nz = zf_ref[2 * e + 1]

            def zstart(j, carry, z0=z0):
                piece(zero_ref, 0, z0 + j * MOE_PIECE).start()
                return carry

            lax.fori_loop(0, nz, zstart, 0)
            n_zero = n_zero + nz

        def zwait(j, carry):
            piece(zero_ref, 0, 0).wait()
            return carry

        lax.fori_loop(0, n_zero, zwait, 0)

        def tile_copy(t):
            return pltpu.make_async_copy(
                ztile_ref, xs_hbm.at[pl.ds(pl.multiple_of(t * tf, tf), tf), :], sem)

        t0 = zf_ref[2 * n_exp]
        t1 = zf_ref[2 * n_exp + 1]

        def tstart(t, carry):
            tile_copy(t).start()
            return carry

        def twait(t, carry):
            tile_copy(t).wait()
            return carry

        lax.fori_loop(t0, t1, tstart, 0)
        lax.fori_loop(t0, t1, twait, 0)


def _expert_kernel(te_ref, na_ref, xs_ref, wg_ref, wu_ref, wd_ref, ys_ref, *, d, blocks):
    active = pl.program_id(0) < na_ref[0]

    @pl.when(jnp.logical_not(active))
    def _():
        ys_ref[...] = jnp.zeros_like(ys_ref)

    @pl.when(active)
    def _():
        hb = xs_ref[:, 0:d]
        gate = jnp.sum(xs_ref[:, d:d + MOE_META].astype(F32), axis=-1, keepdims=True)
        y = None
        for lo, hi in blocks:
            a = (_silu(_dot(hb, wg_ref[:, lo:hi])) * _dot(hb, wu_ref[:, lo:hi])).astype(BF16)
            t = _dot(a, wd_ref[lo:hi, :])
            y = t if y is None else y + t
        ys_ref[...] = (y * gate).astype(BF16)


def _combine_kernel(seg_ref, x_ref, mod_ref, meta_ref, ys_hbm, lg_ref, lb_ref, o_ref, ysl_ref, sem,
                    *, d, alpha, n_exp, tm, ls):
    i = pl.program_id(0)

    def piece(src_row, dst_row):
        return pltpu.make_async_copy(
            ys_hbm.at[pl.ds(pl.multiple_of(src_row, MOE_PIECE), MOE_PIECE), :],
            ysl_ref.at[pl.ds(pl.multiple_of(dst_row, MOE_PIECE), MOE_PIECE), :], sem)

    lo = jnp.int32(0)
    for e in range(n_exp):
        base = seg_ref[i, e]
        npc = (seg_ref[i, n_exp + e] + (MOE_PIECE - 1)) // MOE_PIECE

        def start(j, carry, base=base, lo=lo):
            piece(base + j * MOE_PIECE, lo + j * MOE_PIECE).start()
            return carry

        lax.fori_loop(0, npc, start, 0)
        lo = lo + npc * MOE_PIECE
    used = lo // MOE_PIECE

    def clear(j, carry):
        ysl_ref[pl.ds(pl.multiple_of(j * MOE_PIECE, MOE_PIECE), MOE_PIECE), :] = jnp.zeros(
            (MOE_PIECE, d), BF16)
        return carry

    lax.fori_loop(used, ls // MOE_PIECE, clear, 0)

    def wait(j, carry):
        piece(0, 0).wait()
        return carry

    lax.fori_loop(0, used, wait, 0)

    lp1 = meta_ref[:, 0:1]
    lp2 = meta_ref[:, 1:2]
    col = lax.broadcasted_iota(jnp.int32, (tm, ls), 1).astype(F32)
    perm_t = jnp.where(col == lp1, 1.0, jnp.where(col == lp2, 1.0, 0.0)).astype(BF16)
    y = _dot(perm_t, ysl_ref[...])
    gate = mod_ref[0, :, 5 * d:6 * d]
    xn = alpha * x_ref[...] + gate * y
    o_ref[...] = _ln(xn) * lg_ref[...] + lb_ref[...]


def _moe(st, x, mod3, wr_pad, wg, wu, wd, lg, lb, alpha):
    d = x.shape[1]
    n6 = mod3.shape[2]
    n_exp, _, ff = wg.shape
    tm, n_tiles, n_tok = st.tm, st.n_tiles, st.n_tok
    ls = _moe_local_rows(tm, n_exp)
    tf = MOE_ROW_TILE
    dx = d + MOE_META
    i32 = jnp.int32

    hb, rt, cnt = pl.pallas_call(
        functools.partial(_router_kernel, d=d, n_exp=n_exp, tm=tm),
        grid=(n_tiles,),
        in_specs=[st.tok_spec(d), st.mod_spec(n6), _resident((d, LANES))],
        out_specs=[st.tok_spec(d),
                   pl.BlockSpec((None, ROUTE_ROWS, tm), lambda i: (i, 0, 0)),
                   pl.BlockSpec(memory_space=pltpu.SMEM)],
        out_shape=[jax.ShapeDtypeStruct((n_tok, d), BF16),
                   jax.ShapeDtypeStruct((n_tiles, ROUTE_ROWS, tm), F32),
                   jax.ShapeDtypeStruct((n_tiles, n_exp), i32)],
        compiler_params=_cparams("arbitrary"),
        name="moe_router",
    )(x, mod3, wr_pad)

    padded = -(-cnt // MOE_PIECE) * MOE_PIECE
    per_e = jnp.sum(padded, axis=0)
    per_e_full = -(-per_e // tf) * tf
    region_end = jnp.cumsum(per_e_full)
    region_start = region_end - per_e_full
    base = region_start[None, :] + jnp.cumsum(padded, axis=0) - padded
    seg = jnp.concatenate([base, cnt], axis=1).astype(i32)
    n_row_tiles = -(-(TOP_K * n_tok + n_tiles * n_exp * (MOE_PIECE - 1) + n_exp * (tf - 1)) // tf)
    tile_end = region_end // tf
    n_act = tile_end[-1]
    zfill = jnp.concatenate([
        jnp.stack([region_start + per_e, (per_e_full - per_e) // MOE_PIECE], axis=1).reshape(-1),
        jnp.stack([n_act, jnp.asarray(n_row_tiles, i32)])]).astype(i32)
    t_idx = jnp.minimum(jnp.arange(n_row_tiles, dtype=i32), n_act - 1)
    t_e = jnp.sum(t_idx[:, None] >= tile_end[None, :], axis=1).astype(i32)
    upper = (jnp.arange(tm)[:, None] < jnp.arange(tm)[None, :]).astype(BF16)

    xs, meta = pl.pallas_call(
        functools.partial(_dispatch_kernel, d=d, n_exp=n_exp, tm=tm, ls=ls, n_tiles=n_tiles, tf=tf),
        grid_spec=pltpu.PrefetchScalarGridSpec(
            num_scalar_prefetch=2,
            grid=(n_tiles,),
            in_specs=[st.tok_spec(d),
                      pl.BlockSpec((None, ROUTE_ROWS, tm), lambda i, *_: (i, 0, 0)),
                      _resident((tm, tm))],
            out_specs=[pl.BlockSpec(memory_space=pl.ANY), st.tok_spec(LANES)],
            scratch_shapes=[pltpu.VMEM((ls, dx), BF16), pltpu.VMEM((MOE_PIECE, dx), BF16),
                            pltpu.VMEM((tf, dx), BF16), pltpu.SemaphoreType.DMA(())],
        ),
        out_shape=[jax.ShapeDtypeStruct((n_row_tiles * tf, dx), BF16),
                   jax.ShapeDtypeStruct((n_tok, LANES), F32)],
        compiler_params=_cparams("arbitrary"),
        name="moe_dispatch",
    )(seg, zfill, hb, rt, upper)

    ys = pl.pallas_call(
        functools.partial(_expert_kernel, d=d, blocks=_ff_blocks(ff)),
        grid_spec=pltpu.PrefetchScalarGridSpec(
            num_scalar_prefetch=2,
            grid=(n_row_tiles,),
            in_specs=[pl.BlockSpec((tf, dx), lambda i, te, na: (i, 0)),
                      pl.BlockSpec((None, d, ff), lambda i, te, na: (te[i], 0, 0)),
                      pl.BlockSpec((None, d, ff), lambda i, te, na: (te[i], 0, 0)),
                      pl.BlockSpec((None, ff, d), lambda i, te, na: (te[i], 0, 0))],
            out_specs=pl.BlockSpec((tf, d), lambda i, te, na: (i, 0)),
        ),
        out_shape=jax.ShapeDtypeStruct((n_row_tiles * tf, d), BF16),
        compiler_params=_cparams("arbitrary"),
        name="moe_experts",
    )(t_e, n_act.reshape(1).astype(i32), xs, wg, wu, wd)

    return pl.pallas_call(
        functools.partial(_combine_kernel, d=d, alpha=alpha, n_exp=n_exp, tm=tm, ls=ls),
        grid_spec=pltpu.PrefetchScalarGridSpec(
            num_scalar_prefetch=1,
            grid=(n_tiles,),
            in_specs=[st.tok_spec(d), st.mod_spec(n6), st.tok_spec(LANES),
                      pl.BlockSpec(memory_space=pl.ANY), _resident((1, d)), _resident((1, d))],
            out_specs=st.tok_spec(d),
            scratch_shapes=[pltpu.VMEM((ls, d), BF16), pltpu.SemaphoreType.DMA(())],
        ),
        out_shape=jax.ShapeDtypeStruct((n_tok, d), F32),
        compiler_params=_cparams("arbitrary"),
        name="moe_combine",
    )(seg, x, mod3, meta, ys, lg, lb)


def kernel(x, c, ctx, c_ctx, w_mod, b_mod, w_in, w_gk2, b_gk, gla_g, pool_w, pool_scale, sgu_ln_g, sgu_ln_b, sgu_w, sgu_b, w_gate, w_branch, w_o, ln_g, ln_b, w_ffn_gate, w_ffn_up, w_ffn_down, w_router, w_exp_gate, w_exp_up, w_exp_down):
    bsz, t, d = x.shape
    tc = ctx.shape[1]
    depth = w_mod.shape[0]
    n6 = w_mod.shape[2]
    alpha = (2 * depth) ** 0.25
    rows = t // GRID_W
    ng = len(POOL_WINDOWS)
    gk_rank = w_gk2.shape[2]
    assert d == 1024 and w_in.shape[2] == 1792 + 2 * gk_rank and 2 * gk_rank <= LANES
    assert t % GLA_C == 0 and tc % GLA_C == 0 and bsz % POOL_PACK == 0

    n_rows = -(-(bsz + 1) // SUBLANES) * SUBLANES
    cvec = jnp.zeros((n_rows, d), F32).at[:bsz].set(c).at[bsz].set(c_ctx)
    mod_all = _modulation(cvec, w_mod, b_mod)

    lat = _lat_stream(bsz, t)
    cst = _ctx_stream(bsz, tc)
    xl = x.reshape(bsz * t, d)
    xc = ctx.reshape(bsz * tc, d)

    pool_tab_l = _pool_tables(t, rows)
    pool_tab_c = _pool_tables(tc, None)
    chan_tabs = _fourier_chan_tables(ng)
    dft_l = tuple(a.astype(BF16) for a in _dft_tables(t))
    dft_c = tuple(a.astype(BF16) for a in _dft_tables(tc))

    gk0 = 768
    gk1 = gk0 + 2 * gk_rank

    for l in range(depth):
        last = l == depth - 1
        mod3 = mod_all[l].reshape(n_rows, 1, n6)
        win = jnp.concatenate(
            [w_in[l][:, :gk0], w_in[l][:, gk1:], w_in[l][:, gk0:gk1],
             jnp.zeros((d, LANES - 2 * gk_rank), F32)], axis=1).astype(BF16)
        wgk2 = jnp.zeros((LANES, 256), F32)
        wgk2 = wgk2.at[0:gk_rank, 0:128].set(w_gk2[l, 0]).at[gk_rank:2 * gk_rank, 128:256].set(w_gk2[l, 1])
        wgk2 = wgk2.astype(BF16)
        bgk = b_gk[l].reshape(1, 256)
        eye_p = jnp.eye(POOL_PACK, dtype=F32)
        pw_bd = jnp.stack([jnp.kron(eye_p, pool_w[l, g]) for g in range(ng)]).astype(BF16)
        ps_t = jnp.tile(pool_scale[l].reshape(ng, 1, 256 // ng), (1, 1, POOL_PACK))
        sgu_wb = sgu_w[l].astype(BF16)
        sgu_bm = jnp.repeat(sgu_b[l].T, 256 // ng, axis=1)
        wg = w_gate[l].astype(BF16)
        wb = w_branch[l].astype(BF16)
        wo = w_o[l].astype(BF16)
        lg0, lb0 = ln_g[l, 0].reshape(1, d), ln_b[l, 0].reshape(1, d)
        lg1, lb1 = ln_g[l, 1].reshape(1, d), ln_b[l, 1].reshape(1, d)

        hl, qk_l, v_l, r_l, p_l, f_l, s_l = _inproj(lat, xl, mod3, win, wgk2, bgk)
        hc, qk_c, v_c, r_c, p_c, f_c, s_c = _inproj(cst, xc, mod3, win, wgk2, bgk)
        ya_l, ya_c = _gla(bsz, t, tc, (qk_l, v_l, r_l), (qk_c, v_c, r_c), gla_g[l].reshape(1, 256))
        sg = (sgu_ln_g[l].reshape(1, 256), sgu_ln_b[l].reshape(1, 256), sgu_wb, sgu_bm)
        yb_l = _pool(bsz, t, p_l, pool_tab_l, pw_bd, ps_t)
        yc_l = _fourier(bsz, t, f_l, chan_tabs, dft_l)
        yd_l = _sgu(lat, s_l, *sg)
        xl = _merge(lat, xl, hl, (ya_l, yb_l, yc_l, yd_l), mod3, wg, wb, wo, lg0, lb0, alpha)
        if not last:
            yb_c = _pool(bsz, tc, p_c, pool_tab_c, pw_bd, ps_t)
            yc_c = _fourier(bsz, tc, f_c, chan_tabs, dft_c)
            yd_c = _sgu(cst, s_c, *sg)
            xc = _merge(cst, xc, hc, (ya_c, yb_c, yc_c, yd_c), mod3, wg, wb, wo, lg0, lb0, alpha)

        i = l // 2
        if l % 2 == 0:
            fw = (w_ffn_gate[i].astype(BF16), w_ffn_up[i].astype(BF16), w_ffn_down[i].astype(BF16))
            xl = _ffn(lat, xl, mod3, *fw, lg1, lb1, alpha)
            if not last:
                xc = _ffn(cst, xc, mod3, *fw, lg1, lb1, alpha)
        else:
            n_exp = w_router.shape[2]
            wr_pad = jnp.zeros((d, LANES), F32).at[:, :n_exp].set(w_router[i])
            ew = (w_exp_gate[i].astype(BF16), w_exp_up[i].astype(BF16), w_exp_down[i].astype(BF16))
            xl = _moe(lat, xl, mod3, wr_pad, *ew, lg1, lb1, alpha)
            if not last:
                xc = _moe(cst, xc, mod3, wr_pad, *ew, lg1, lb1, alpha)
    return xl.reshape(bsz, t, d)
```

```python
import functools
import math

import jax
import jax.numpy as jnp
from jax import lax
from jax.experimental import pallas as pl
from jax.experimental.pallas import tpu as pltpu

F32 = jnp.float32
BF16 = jnp.bfloat16

GRID_W = 64
GLA_HEADS = 4
GK_TAU = 16.0
POOL_WINDOWS = (2, 4, 8, 16)
SGU_CHUNK = 128
TOP_K = 2
EPS = 1e-6

LANES = 128
SUBLANES = 8
VMEM_LIMIT_BYTES = 56 * 1024 * 1024

GLA_C = 128


def _cparams(*sem):
    return pltpu.CompilerParams(dimension_semantics=sem, vmem_limit_bytes=VMEM_LIMIT_BYTES)


def _resident(shape):
    nd = len(shape)
    return pl.BlockSpec(shape, lambda *_: (0,) * nd, pipeline_mode=pl.Buffered(1))


class _Sel:
    def __init__(self, arr, *idx):
        self.arr, self.idx = arr, tuple(idx)

    @property
    def shape(self):
        return tuple(self.arr.shape[len(self.idx):])

    def spec(self):
        idx, tail = self.idx, self.shape
        return pl.BlockSpec((None,) * len(idx) + tail, lambda *_: idx + (0,) * len(tail),
                            pipeline_mode=pl.Buffered(1))


def _dot(a, b):
    return jnp.dot(a, b, preferred_element_type=F32)


def _dot_nt(a, b):
    return lax.dot_general(a, b, (((1,), (1,)), ((), ())), preferred_element_type=F32)


def _dot_tn(a, b):
    return lax.dot_general(a, b, (((0,), (0,)), ((), ())), preferred_element_type=F32)


def _split3(x):
    hi = x.astype(BF16)
    r1 = x - hi.astype(F32)
    mid = r1.astype(BF16)
    lo = (r1 - mid.astype(F32)).astype(BF16)
    return hi, mid, lo


def _dot_exact_rhs(x, w):
    hi, mid, lo = _split3(x)
    return _dot(hi, w) + _dot(mid, w) + _dot(lo, w)


def _dot_exact_lhs(w, x):
    hi, mid, lo = _split3(x)
    return _dot(w, hi) + _dot(w, mid) + _dot(w, lo)


def _ln(x):
    mu = jnp.mean(x, axis=-1, keepdims=True)
    xc = x - mu
    var = jnp.mean(xc * xc, axis=-1, keepdims=True)
    return xc * lax.rsqrt(var + EPS)


def _sigmoid(x):
    return 1.0 / (1.0 + jnp.exp(-x))


def _silu(x):
    return x * _sigmoid(x)


def _gelu_tanh(x):
    return 0.5 * x * (1.0 + jnp.tanh(math.sqrt(2.0 / math.pi) * (x + 0.044715 * (x * x * x))))


def _pick_tile(n, candidates=(1024, 512, 256, 128)):
    for c in candidates:
        if n % c == 0:
            return c
    raise ValueError(f"no tile for {n}")


def _mod_kernel(c_ref, w_ref, b_ref, o_ref):
    s = _silu(c_ref[...])
    sh, sm, sl = _split3(s)
    w = w_ref[0]
    wh = w.astype(BF16)
    wl = (w - wh.astype(F32)).astype(BF16)
    acc = _dot(sh, wh) + (_dot(sh, wl) + _dot(sm, wh)) + (_dot(sm, wl) + _dot(sl, wh))
    o_ref[0] = acc + b_ref[0]


def _modulation(cvec, w_mod, b_mod):
    depth, d, n6 = w_mod.shape
    rows = cvec.shape[0]
    tn = _pick_tile(n6, (1536, 1024, 512, 256, 128))
    return pl.pallas_call(
        _mod_kernel,
        grid=(depth, n6 // tn),
        in_specs=[
            pl.BlockSpec((rows, d), lambda l, j: (0, 0)),
            pl.BlockSpec((1, d, tn), lambda l, j: (l, 0, j)),
            pl.BlockSpec((1, 1, tn), lambda l, j: (l, 0, j)),
        ],
        out_specs=pl.BlockSpec((1, rows, tn), lambda l, j: (l, 0, j)),
        out_shape=jax.ShapeDtypeStruct((depth, rows, n6), F32),
        compiler_params=_cparams("parallel", "parallel"),
        name="modulation",
    )(cvec, w_mod, b_mod.reshape(depth, 1, n6))


class _Stream:
    def __init__(self, n_tok, seq, mod_row_of_tile, tm):
        self.n_tok, self.seq, self.tm = n_tok, seq, tm
        self.mod_row_of_tile = mod_row_of_tile
        self.n_tiles = n_tok // tm

    def mod_spec(self, mod):
        f = self.mod_row_of_tile
        (l,) = mod.idx
        return pl.BlockSpec((None, 1, 1, mod.shape[-1]), lambda i, *_: (l, f(i), 0, 0))

    def tok_spec(self, width):
        return pl.BlockSpec((self.tm, width), lambda i, *_: (i, 0))


def _lat_stream(bsz, t):
    tm = _pick_tile(t, (512, 256, 128))
    per = t // tm
    return _Stream(bsz * t, t, lambda i: i // per, tm)


def _ctx_stream(bsz, tc):
    tm = _pick_tile(bsz * tc, (512, 256, 128))
    return _Stream(bsz * tc, tc, lambda i: bsz, tm)


def _inproj_kernel(x_ref, mod_ref, win_ref, wgk2_ref, bgk_ref,
                   h_ref, qkla_ref, v_ref, r_ref, pool_ref, fft_ref, sgu_ref, *, d, q_scale):
    x = x_ref[...]
    shift = mod_ref[0, :, 0:d]
    scale = mod_ref[0, :, d:2 * d]
    hb = (_ln(x) * (1.0 + scale) + shift).astype(BF16)
    h_ref[...] = hb
    p = _dot(hb, win_ref[...])
    qkla_ref[:, 0:128] = p[:, 0:128] * q_scale
    qkla_ref[:, 128:256] = p[:, 128:256]
    v_ref[...] = p[:, 256:512].astype(BF16)
    r_ref[...] = p[:, 512:768].astype(BF16)
    pool_ref[...] = p[:, 768:1024].astype(BF16)
    fft_ref[...] = p[:, 1024:1280].astype(BF16)
    sgu_ref[...] = p[:, 1280:1792].astype(BF16)
    z = _dot(p[:, 1792:1920].astype(BF16), wgk2_ref[...]) + bgk_ref[...]
    la = (jnp.minimum(z, 0.0) - jnp.log(1.0 + jnp.exp(-jnp.abs(z)))) * (1.0 / GK_TAU)
    c = GLA_C
    tri = jnp.where(lax.broadcasted_iota(jnp.int32, (c, c), 1)
                    <= lax.broadcasted_iota(jnp.int32, (c, c), 0), 1.0, 0.0).astype(BF16)
    split = jnp.concatenate(_split3(la), axis=-1)
    for ch in range(x.shape[0] // c):
        rows = slice(ch * c, (ch + 1) * c)
        pp = _dot(tri, split[rows, :])
        pref = (pp[:, 0:256] + pp[:, 256:512]) + pp[:, 512:768]
        qkla_ref[rows, 256:384] = pref[:, 0:128]
        pb = pref[:, 128:256]
        qkla_ref[rows, 384:512] = pb[c - 1:c, :] - pb + la[rows, 128:256]


def _inproj(st, x, mod, win, wgk2, bgk):
    d = x.shape[1]
    outs = [
        jax.ShapeDtypeStruct((st.n_tok, d), BF16),
        jax.ShapeDtypeStruct((st.n_tok, 512), F32),
        jax.ShapeDtypeStruct((st.n_tok, 256), BF16),
        jax.ShapeDtypeStruct((st.n_tok, 256), BF16),
        jax.ShapeDtypeStruct((st.n_tok, 256), BF16),
        jax.ShapeDtypeStruct((st.n_tok, 256), BF16),
        jax.ShapeDtypeStruct((st.n_tok, 512), BF16),
    ]
    dk = 128 // GLA_HEADS
    return pl.pallas_call(
        functools.partial(_inproj_kernel, d=d, q_scale=dk ** -0.5),
        grid=(st.n_tiles,),
        in_specs=[st.tok_spec(d), st.mod_spec(mod), win.spec(), wgk2.spec(), bgk.spec()],
        out_specs=[st.tok_spec(o.shape[1]) for o in outs],
        out_shape=outs,
        compiler_params=_cparams("parallel"),
        name="in_proj",
    )(x, mod.arr, win.arr, wgk2.arr, bgk.arr)


def _gla_kernel(qk_l, v_l, r_l, qk_c, v_c, r_c, g_ref, y_l, y_c, ht, oacc_l, oacc_c,
                *, nch_l, nch_c):
    c = GLA_C
    nh = GLA_HEADS
    ii4 = lax.broadcasted_iota(jnp.int32, (nh * c, c), 0) & (c - 1)
    jj4 = lax.broadcasted_iota(jnp.int32, (nh * c, c), 1)
    lane_qk = lax.broadcasted_iota(jnp.int32, (c, 128), 1) // (128 // nh)
    lane_v = lax.broadcasted_iota(jnp.int32, (c, 256), 1) // (256 // nh)
    st_mask = (lax.broadcasted_iota(jnp.int32, (256, 128), 0) // (256 // nh)
               == lax.broadcasted_iota(jnp.int32, (256, 128), 1) // (128 // nh))

    def chunk(qk, v, oacc, idx, fwd):
        dirn = 0 if fwd else 1
        pair4 = (jj4 <= ii4) if fwd else (jj4 > ii4)
        mid, last = (c // 2 - 1, c - 1) if fwd else (c // 2, 0)
        la0 = 256 if fwd else 384
        rows = pl.ds(pl.multiple_of(idx * c, c), c)
        q = qk[rows, 0:128]
        k = qk[rows, 128:256]
        b = qk[rows, la0:la0 + 128]
        b_mid = b[mid:mid + 1, :]
        b_last = b[last:last + 1, :]
        qe = q * jnp.exp(b - b_mid)
        ke = (k * jnp.exp(b_mid - b)).astype(BF16)
        kt = (k * jnp.exp(b_last - b)).astype(BF16)
        qa = (qe * jnp.exp(b_mid)).astype(BF16)
        qs = jnp.concatenate([jnp.where(lane_qk == h, qe, 0.0) for h in range(nh)],
                             axis=0).astype(BF16)
        att = jnp.where(pair4, _dot_nt(qs, ke), 0.0).astype(BF16)
        vv = v[rows, :]
        os_ = _dot(att, vv)
        o = _dot_nt(qa, ht[dirn].astype(BF16))
        for h in range(nh):
            o = o + jnp.where(lane_v == h, os_[h * c:(h + 1) * c, :], 0.0)
        oacc[dirn, rows, :] = o
        dht = _dot_tn(vv, kt)
        ht[dirn] = ht[dirn] * jnp.exp(b_last) + jnp.where(st_mask, dht, 0.0)

    def run(qk, v, oacc, nch):
        def step(n, carry):
            chunk(qk, v, oacc, n, True)
            chunk(qk, v, oacc, nch - 1 - n, False)
            return carry

        lax.fori_loop(0, nch, step, 0)

    ht[...] = jnp.zeros_like(ht)
    run(qk_c, v_c, oacc_c, nch_c)
    run(qk_l, v_l, oacc_l, nch_l)

    hv = 256 // nh
    mh = jnp.where(lax.broadcasted_iota(jnp.int32, (256, 256), 0) // hv
                   == lax.broadcasted_iota(jnp.int32, (256, 256), 1) // hv, 1.0 / hv, 0.0).astype(BF16)
    g = g_ref[...]

    def finish(oacc, r, y, nch):
        def step(n, carry):
            rows = pl.ds(pl.multiple_of(n * c, c), c)
            o = oacc[0, rows, :] + oacc[1, rows, :]
            sq = o * o
            sh = sq.astype(BF16)
            sl = (sq - sh.astype(F32)).astype(BF16)
            ms = _dot(sh, mh) + _dot(sl, mh)
            rr = r[rows, :].astype(F32)
            y[rows, :] = (o * lax.rsqrt(ms + EPS) * g * _silu(rr)).astype(BF16)
            return carry

        lax.fori_loop(0, nch, step, 0)

    finish(oacc_c, r_c, y_c, nch_c)
    finish(oacc_l, r_l, y_l, nch_l)


def _gla(bsz, t, tc, lat, ctx, gla_g):
    c = GLA_C
    nch_l, nch_c = t // c, tc // c

    def prep(arrs, seq):
        qkla, v, r = arrs
        return (qkla.reshape(bsz, seq, 512), v.reshape(bsz, seq, 256), r.reshape(bsz, seq, 256))

    def specs(seq):
        return [pl.BlockSpec((None, seq, 512), lambda b: (b, 0, 0)),
                pl.BlockSpec((None, seq, 256), lambda b: (b, 0, 0)),
                pl.BlockSpec((None, seq, 256), lambda b: (b, 0, 0))]

    y_l, y_c = pl.pallas_call(
        functools.partial(_gla_kernel, nch_l=nch_l, nch_c=nch_c),
        grid=(bsz,),
        in_specs=specs(t) + specs(tc) + [gla_g.spec()],
        out_specs=[pl.BlockSpec((None, t, 256), lambda b: (b, 0, 0)),
                   pl.BlockSpec((None, tc, 256), lambda b: (b, 0, 0))],
        out_shape=[jax.ShapeDtypeStruct((bsz, t, 256), BF16),
                   jax.ShapeDtypeStruct((bsz, tc, 256), BF16)],
        scratch_shapes=[pltpu.VMEM((2, 256, 128), F32), pltpu.VMEM((2, t, 256), F32),
                        pltpu.VMEM((2, tc, 256), F32)],
        compiler_params=_cparams("parallel"),
        name="gla",
    )(*prep(lat, t), *prep(ctx, tc), gla_g.arr)
    return y_l.reshape(bsz * t, 256), y_c.reshape(bsz * tc, 256)


POOL_PACK = 4


def _pool_kernel(x_ref, s_ref, ic_ref, pw_ref, ps_ref, o_ref, xp_ref, op_ref, *, ng, seq):
    g = pl.program_id(1)
    grp = 256 // ng
    rc = min(seq, 512)

    @pl.when(g == 0)
    def _():
        def pack(c, carry):
            rows = pl.ds(pl.multiple_of(c * rc, rc), rc)
            for gg in range(ng):
                xp_ref[gg, rows, :] = jnp.concatenate(
                    [x_ref[b, rows, gg * grp:(gg + 1) * grp] for b in range(POOL_PACK)], axis=-1)
            return carry

        lax.fori_loop(0, seq // rc, pack, 0)

    x = xp_ref[g]
    ic = ic_ref[...]
    m = _dot(s_ref[...], x) * jnp.concatenate([ic, ic], axis=-1)
    dlt = (m - x.astype(F32)).astype(BF16)
    op_ref[g] = (_dot(dlt, pw_ref[...]) * ps_ref[...]).astype(BF16)

    @pl.when(g == ng - 1)
    def _():
        def unpack(c, carry):
            rows = pl.ds(pl.multiple_of(c * rc, rc), rc)
            for b in range(POOL_PACK):
                o_ref[b, rows, :] = jnp.concatenate(
                    [op_ref[gg, rows, b * grp:(b + 1) * grp] for gg in range(ng)], axis=-1)
            return carry

        lax.fori_loop(0, seq // rc, unpack, 0)


def _window_bounds(pos, n, win):
    return jnp.clip(pos - win // 2, 0, n), jnp.clip(pos + (win - win // 2), 0, n)


def _pool_tables(seq, rows):
    mats, invs = [], []
    tok = jnp.arange(seq)
    for win in POOL_WINDOWS:
        if rows is None:
            lo, hi = _window_bounds(tok, seq, win)
            member = (tok[None, :] >= lo[:, None]) & (tok[None, :] < hi[:, None])
            cnt = hi - lo
        else:
            rr, cc = tok // GRID_W, tok % GRID_W
            lo_r, hi_r = _window_bounds(rr, rows, win)
            lo_c, hi_c = _window_bounds(cc, GRID_W, win)
            member = ((rr[None, :] >= lo_r[:, None]) & (rr[None, :] < hi_r[:, None])
                      & (cc[None, :] >= lo_c[:, None]) & (cc[None, :] < hi_c[:, None]))
            cnt = (hi_r - lo_r) * (hi_c - lo_c)
        mats.append(member.astype(BF16))
        invs.append(jnp.broadcast_to((1.0 / cnt.astype(F32))[:, None], (seq, LANES)))
    return jnp.stack(mats), jnp.stack(invs)


def _pool(bsz, seq, xp, tables, pw_bd, ps_t, l):
    smat, inv = tables
    ng = len(POOL_WINDOWS)
    nq = bsz // POOL_PACK
    blk = pl.BlockSpec((POOL_PACK, seq, 256), lambda q, g: (q, 0, 0))
    out = pl.pallas_call(
        functools.partial(_pool_kernel, ng=ng, seq=seq),
        grid=(nq, ng),
        in_specs=[blk,
                  pl.BlockSpec((None, seq, seq), lambda q, g: (g, 0, 0)),
                  pl.BlockSpec((None, seq, LANES), lambda q, g: (g, 0, 0)),
                  pl.BlockSpec((None, None, 256, 256), lambda q, g: (l, g, 0, 0)),
                  pl.BlockSpec((None, None, 1, 256), lambda q, g: (l, g, 0, 0))],
        out_specs=blk,
        out_shape=jax.ShapeDtypeStruct((bsz, seq, 256), BF16),
        scratch_shapes=[pltpu.VMEM((ng, seq, 256), BF16), pltpu.VMEM((ng, seq, 256), BF16)],
        compiler_params=_cparams("arbitrary", "arbitrary"),
        name="pool",
    )(xp.reshape(bsz, seq, 256), smat, inv, pw_bd, ps_t)
    return out.reshape(bsz * seq, 256)


def _fourier_kernel(z_ref, cb_ref, sb_ref, ct_ref, st_ref, o_ref, *, scale):
    z = z_ref[...]
    zc = _dot(z, cb_ref[...]).astype(BF16)
    zs = _dot(z, sb_ref[...]).astype(BF16)
    y = _dot(ct_ref[...], zc) - _dot(st_ref[...], zs)
    o_ref[...] = (y * scale).astype(BF16)


def _dft_tables(n):
    k = jnp.arange(n, dtype=jnp.int32)
    ang = ((k[:, None] * k[None, :]) % n).astype(F32) * (2.0 * math.pi / n)
    return jnp.cos(ang), jnp.sin(ang)


def _fourier_chan_tables(ng):
    grp = 256 // ng
    cg, sg = _dft_tables(grp)
    eye = jnp.eye(ng, dtype=F32)
    return jnp.kron(eye, cg).astype(BF16), jnp.kron(eye, sg).astype(BF16)


def _fourier(bsz, seq, z, chan_tabs, seq_tabs):
    cb, sb = chan_tabs
    ct, st = seq_tabs
    grp = cb.shape[0] // 4
    out = pl.pallas_call(
        functools.partial(_fourier_kernel, scale=1.0 / math.sqrt(seq * grp)),
        grid=(bsz,),
        in_specs=[pl.BlockSpec((None, seq, 256), lambda b: (b, 0, 0)),
                  _resident((256, 256)), _resident((256, 256)),
                  _resident((seq, seq)), _resident((seq, seq))],
        out_specs=pl.BlockSpec((None, seq, 256), lambda b: (b, 0, 0)),
        out_shape=jax.ShapeDtypeStruct((bsz, seq, 256), BF16),
        compiler_params=_cparams("parallel"),
        name="fourier",
    )(z.reshape(bsz, seq, 256), cb, sb, ct, st)
    return out.reshape(bsz * seq, 256)


def _sgu_kernel(s_ref, lg_ref, lb_ref, w_ref, bm_ref, o_ref, *, n_chunks, ng):
    u = _gelu_tanh(s_ref[:, 0:256].astype(F32))
    v = _gelu_tanh(s_ref[:, 256:512].astype(F32))
    vb = (_ln(v) * lg_ref[...] + lb_ref[...]).astype(BF16)
    lane_g = lax.broadcasted_iota(jnp.int32, (SGU_CHUNK, 256), 1) // (256 // ng)
    for ch in range(n_chunks):
        rows = slice(ch * SGU_CHUNK, (ch + 1) * SGU_CHUNK)
        vc = vb[rows, :]
        s = bm_ref[...]
        for g in range(ng):
            s = s + jnp.where(lane_g == g, _dot(w_ref[g], vc), 0.0)
        o_ref[rows, :] = (u[rows, :] * s).astype(BF16)


def _sgu(st, sgu_in, lg, lb, w, bm):
    ng = w.shape[0]
    tm = st.tm
    return pl.pallas_call(
        functools.partial(_sgu_kernel, n_chunks=tm // SGU_CHUNK, ng=ng),
        grid=(st.n_tiles,),
        in_specs=[st.tok_spec(512), lg.spec(), lb.spec(), w.spec(), bm.spec()],
        out_specs=st.tok_spec(256),
        out_shape=jax.ShapeDtypeStruct((st.n_tok, 256), BF16),
        compiler_params=_cparams("parallel"),
        name="sgu",
    )(sgu_in, lg.arr, lb.arr, w.arr, bm.arr)


MERGE_COLS = 512


def _merge_kernel(x_ref, h_ref, y0, y1, y2, y3, mod_ref, wg_ref, wb_ref, wo_ref, lg_ref, lb_ref,
                  o_ref, *, d, alpha):
    h = h_ref[...]
    ys = (y0[...], y1[...], y2[...], y3[...])
    parts = []
    for cb in range(d // MERGE_COLS):
        cols = slice(cb * MERGE_COLS, (cb + 1) * MERGE_COLS)
        m = None
        for i in range(4):
            t = _sigmoid(_dot(h, wg_ref[i, :, cols])) * _dot(ys[i], wb_ref[i, :, cols])
            m = t if m is None else m + t
        parts.append(m.astype(BF16))
    y = _dot(jnp.concatenate(parts, axis=-1), wo_ref[...])
    gate = mod_ref[0, :, 2 * d:3 * d]
    xn = alpha * x_ref[...] + gate * y
    o_ref[...] = _ln(xn) * lg_ref[...] + lb_ref[...]


def _merge(st, x, h, ys, mod, wg, wb, wo, lg, lb, alpha):
    d = x.shape[1]
    return pl.pallas_call(
        functools.partial(_merge_kernel, d=d, alpha=alpha),
        grid=(st.n_tiles,),
        in_specs=[st.tok_spec(d), st.tok_spec(d)] + [st.tok_spec(256)] * 4
        + [st.mod_spec(mod), wg.spec(), wb.spec(), wo.spec(), lg.spec(), lb.spec()],
        out_specs=st.tok_spec(d),
        out_shape=jax.ShapeDtypeStruct((st.n_tok, d), F32),
        compiler_params=_cparams("parallel"),
        name="merge",
    )(x, h, *ys, mod.arr, wg.arr, wb.arr, wo.arr, lg.arr, lb.arr)


def _ff_blocks(ff):
    if ff <= 1536 or ff % 256:
        return [(0, ff)]
    first = (ff // 512 + (1 if (ff // 256) % 2 else 0)) * 256
    return [(0, first), (first, ff)]


def _ffn_kernel(x_ref, mod_ref, wg_ref, wu_ref, wd_ref, lg_ref, lb_ref, o_ref, *, d, alpha, blocks):
    x = x_ref[...]
    shift = mod_ref[0, :, 3 * d:4 * d]
    scale = mod_ref[0, :, 4 * d:5 * d]
    gate = mod_ref[0, :, 5 * d:6 * d]
    hb = (_ln(x) * (1.0 + scale) + shift).astype(BF16)
    y = None
    for lo, hi in blocks:
        a = (_silu(_dot(hb, wg_ref[:, lo:hi])) * _dot(hb, wu_ref[:, lo:hi])).astype(BF16)
        t = _dot(a, wd_ref[lo:hi, :])
        y = t if y is None else y + t
    xn = alpha * x + gate * y
    o_ref[...] = _ln(xn) * lg_ref[...] + lb_ref[...]


def _ffn(st, x, mod, wg, wu, wd, lg, lb, alpha):
    d = x.shape[1]
    ff = wg.shape[1]
    return pl.pallas_call(
        functools.partial(_ffn_kernel, d=d, alpha=alpha, blocks=_ff_blocks(ff)),
        grid=(st.n_tiles,),
        in_specs=[st.tok_spec(d), st.mod_spec(mod), wg.spec(), wu.spec(), wd.spec(),
                  lg.spec(), lb.spec()],
        out_specs=st.tok_spec(d),
        out_shape=jax.ShapeDtypeStruct((st.n_tok, d), F32),
        compiler_params=_cparams("parallel"),
        name="ffn",
    )(x, mod.arr, wg.arr, wu.arr, wd.arr, lg.arr, lb.arr)


MOE_PIECE = 16
MOE_ROW_TILE = 512
MOE_META = LANES


def _moe_local_rows(tm, n_exp):
    return -(-(TOP_K * tm + n_exp * (MOE_PIECE - 1)) // LANES) * LANES


ROUTE_ROWS = SUBLANES


def _router_kernel(x_ref, mod_ref, wr_ref, hb_ref, rt_ref, cnt_ref, *, d, n_exp, tm):
    i = pl.program_id(0)
    x = x_ref[...]
    shift = mod_ref[0, :, 3 * d:4 * d]
    scale = mod_ref[0, :, 4 * d:5 * d]
    h = _ln(x) * (1.0 + scale) + shift
    hb = h.astype(BF16)
    hb_ref[...] = hb

    wr = wr_ref[...]
    wrh = wr.astype(BF16)
    wrl = (wr - wrh.astype(F32)).astype(BF16)
    hl = (h - hb.astype(F32)).astype(BF16)
    logits = _dot(hb, wrh) + (_dot(hb, wrl) + _dot(hl, wrh))
    lt = logits.T[0:n_exp, :]
    sub = lax.broadcasted_iota(jnp.int32, (n_exp, tm), 0).astype(F32)
    neg = jnp.float32(-jnp.inf)
    v1 = jnp.max(lt, axis=0, keepdims=True)
    i1 = jnp.min(jnp.where(lt == v1, sub, float(n_exp)), axis=0, keepdims=True)
    rest = jnp.where(sub == i1, neg, lt)
    v2 = jnp.max(rest, axis=0, keepdims=True)
    i2 = jnp.min(jnp.where(rest == v2, sub, float(n_exp)), axis=0, keepdims=True)
    e2 = jnp.exp(v2 - v1)
    w1 = 1.0 / (1.0 + e2)
    w2 = e2 / (1.0 + e2)
    oht = jnp.where(sub == i1, 1.0, jnp.where(sub == i2, 1.0, 0.0))
    cnt_col = jnp.sum(oht, axis=1, keepdims=True)
    for e in range(n_exp):
        cnt_ref[i, e] = cnt_col[e, 0].astype(jnp.int32)
    rec = lax.broadcasted_iota(jnp.int32, (ROUTE_ROWS, tm), 0)
    rt_ref[...] = jnp.where(rec == 0, i1, jnp.where(rec == 1, i2, jnp.where(
        rec == 2, w1, jnp.where(rec == 3, w2, 0.0))))


def _dispatch_kernel(seg_ref, zf_ref, hb_ref, rt_ref, u_ref, xs_hbm, meta_ref,
                     sorted_ref, zero_ref, ztile_ref, sem, *, d, n_exp, tm, ls, n_tiles, tf):
    i = pl.program_id(0)
    hb = hb_ref[...]
    i1 = rt_ref[0:1, :]
    i2 = rt_ref[1:2, :]
    w1 = rt_ref[2:3, :]
    w2 = rt_ref[3:4, :]
    sub = lax.broadcasted_iota(jnp.int32, (n_exp, tm), 0).astype(F32)
    oh1 = sub == i1
    oh2 = sub == i2
    oht = jnp.where(oh1, 1.0, jnp.where(oh2, 1.0, 0.0))
    cumt = _dot(oht.astype(BF16), u_ref[...])
    cnts = [seg_ref[i, n_exp + e] for e in range(n_exp)]
    npc = [(cn + (MOE_PIECE - 1)) // MOE_PIECE for cn in cnts]
    los = []
    lo = jnp.int32(0)
    for e in range(n_exp):
        los.append(lo)
        lo = lo + npc[e] * MOE_PIECE
    lo_b = jnp.zeros((n_exp, tm), F32)
    for e in range(n_exp):
        lo_b = jnp.where(sub == float(e), los[e].astype(F32), lo_b)
    lpt = cumt + lo_b
    lp1 = jnp.sum(jnp.where(oh1, lpt, 0.0), axis=0, keepdims=True)
    lp2 = jnp.sum(jnp.where(oh2, lpt, 0.0), axis=0, keepdims=True)

    row = lax.broadcasted_iota(jnp.int32, (ls, tm), 0).astype(F32)
    m1 = row == lp1
    m2 = row == lp2
    perm = jnp.where(m1, 1.0, jnp.where(m2, 1.0, 0.0)).astype(BF16)
    gcol = jnp.sum(jnp.where(m1, w1, jnp.where(m2, w2, 0.0)), axis=1, keepdims=True)
    slot = i % 2

    def piece(src_view, src_row, dst_row, s):
        return pltpu.make_async_copy(
            src_view.at[pl.ds(pl.multiple_of(src_row, MOE_PIECE), MOE_PIECE), :],
            xs_hbm.at[pl.ds(pl.multiple_of(dst_row, MOE_PIECE), MOE_PIECE), :], sem.at[s])

    def drain(tile):
        n = jnp.int32(0)
        for e in range(n_exp):
            n = n + (seg_ref[tile, n_exp + e] + (MOE_PIECE - 1)) // MOE_PIECE

        def wait(j, carry):
            piece(sorted_ref.at[0], 0, 0, tile % 2).wait()
            return carry

        lax.fori_loop(0, n, wait, 0)

    @pl.when(i >= 2)
    def _():
        drain(i - 2)

    sorted_ref[slot, :, 0:d] = _dot(perm, hb).astype(BF16)
    ghi = gcol.astype(BF16).astype(F32)
    r1 = gcol - ghi
    gmid = r1.astype(BF16).astype(F32)
    glo = r1 - gmid
    lane = lax.broadcasted_iota(jnp.int32, (ls, MOE_META), 1)
    sorted_ref[slot, :, d:d + MOE_META] = jnp.where(
        lane == 0, ghi, jnp.where(lane == 1, gmid, jnp.where(lane == 2, glo, 0.0))).astype(BF16)

    rows8 = jnp.where(sub == 0.0, lp1, jnp.where(sub == 1.0, lp2, 0.0))
    padded = jnp.concatenate([rows8, jnp.zeros((LANES - n_exp, tm), F32)], axis=0)
    meta_ref[...] = padded.T

    for e in range(n_exp):
        base = seg_ref[i, e]

        def start(j, carry, e=e, base=base):
            piece(sorted_ref.at[slot], los[e] + j * MOE_PIECE, base + j * MOE_PIECE, slot).start()
            return carry

        lax.fori_loop(0, npc[e], start, 0)

    @pl.when(i == n_tiles - 1)
    def _():
        @pl.when(i >= 1)
        def _():
            drain(i - 1)

        drain(i)
        zero_ref[...] = jnp.zeros_like(zero_ref)
        ztile_ref[...] = jnp.zeros_like(ztile_ref)
        n_zero = jnp.int32(0)
        for e in range(n_exp):
            z0 = zf_ref[2 * e]
            nz = zf_ref[2 * e + 1]

            def zstart(j, carry, z0=z0):
                piece(zero_ref, 0, z0 + j * MOE_PIECE, 2).start()
                return carry

            lax.fori_loop(0, nz, zstart, 0)
            n_zero = n_zero + nz

        def zwait(j, carry):
            piece(zero_ref, 0, 0, 2).wait()
            return carry

        lax.fori_loop(0, n_zero, zwait, 0)

        def tile_copy(t):
            return pltpu.make_async_copy(
                ztile_ref, xs_hbm.at[pl.ds(pl.multiple_of(t * tf, tf), tf), :], sem.at[2])

        t0 = zf_ref[2 * n_exp]
        t1 = zf_ref[2 * n_exp + 1]

        def tstart(t, carry):
            tile_copy(t).start()
            return carry

        def twait(t, carry):
            tile_copy(t).wait()
            return carry

        lax.fori_loop(t0, t1, tstart, 0)
        lax.fori_loop(t0, t1, twait, 0)


def _expert_kernel(te_ref, na_ref, xs_ref, wg_ref, wu_ref, wd_ref, ys_ref, *, d, blocks):
    active = pl.program_id(0) < na_ref[0]

    @pl.when(jnp.logical_not(active))
    def _():
        ys_ref[...] = jnp.zeros_like(ys_ref)

    @pl.when(active)
    def _():
        hb = xs_ref[:, 0:d]
        gate = jnp.sum(xs_ref[:, d:d + MOE_META].astype(F32), axis=-1, keepdims=True)
        y = None
        for lo, hi in blocks:
            a = (_silu(_dot(hb, wg_ref[:, lo:hi])) * _dot(hb, wu_ref[:, lo:hi])).astype(BF16)
            t = _dot(a, wd_ref[lo:hi, :])
            y = t if y is None else y + t
        ys_ref[...] = (y * gate).astype(BF16)


def _combine_kernel(seg_ref, x_ref, mod_ref, meta_ref, ys_hbm, lg_ref, lb_ref, o_ref, ysl_ref, sem,
                    *, d, alpha, n_exp, tm, ls, n_tiles):
    i = pl.program_id(0)

    def piece(src_row, dst_row, s):
        return pltpu.make_async_copy(
            ys_hbm.at[pl.ds(pl.multiple_of(src_row, MOE_PIECE), MOE_PIECE), :],
            ysl_ref.at[s, pl.ds(pl.multiple_of(dst_row, MOE_PIECE), MOE_PIECE), :], sem.at[s])

    def n_pieces(tile):
        n = jnp.int32(0)
        for e in range(n_exp):
            n = n + (seg_ref[tile, n_exp + e] + (MOE_PIECE - 1)) // MOE_PIECE
        return n

    def fetch(tile):
        s = tile % 2
        lo = jnp.int32(0)
        for e in range(n_exp):
            base = seg_ref[tile, e]
            npc = (seg_ref[tile, n_exp + e] + (MOE_PIECE - 1)) // MOE_PIECE

            def start(j, carry, base=base, lo=lo):
                piece(base + j * MOE_PIECE, lo + j * MOE_PIECE, s).start()
                return carry

            lax.fori_loop(0, npc, start, 0)
            lo = lo + npc * MOE_PIECE

        def clear(j, carry):
            ysl_ref[s, pl.ds(pl.multiple_of(j * MOE_PIECE, MOE_PIECE), MOE_PIECE), :] = jnp.zeros(
                (MOE_PIECE, d), BF16)
            return carry

        lax.fori_loop(lo // MOE_PIECE, ls // MOE_PIECE, clear, 0)

    @pl.when(i == 0)
    def _():
        fetch(i)

    @pl.when(i + 1 < n_tiles)
    def _():
        fetch(i + 1)

    slot = i % 2

    def wait(j, carry):
        piece(0, 0, slot).wait()
        return carry

    lax.fori_loop(0, n_pieces(i), wait, 0)

    lp1 = meta_ref[:, 0:1]
    lp2 = meta_ref[:, 1:2]
    col = lax.broadcasted_iota(jnp.int32, (tm, ls), 1).astype(F32)
    perm_t = jnp.where(col == lp1, 1.0, jnp.where(col == lp2, 1.0, 0.0)).astype(BF16)
    y = _dot(perm_t, ysl_ref[slot])
    gate = mod_ref[0, :, 5 * d:6 * d]
    xn = alpha * x_ref[...] + gate * y
    o_ref[...] = _ln(xn) * lg_ref[...] + lb_ref[...]


def _moe(st, x, mod, wr_pad, wg, wu, wd, lg, lb, alpha):
    d = x.shape[1]
    n_exp, _, ff = wg.shape
    (li,) = wg.idx
    tm, n_tiles, n_tok = st.tm, st.n_tiles, st.n_tok
    ls = _moe_local_rows(tm, n_exp)
    tf = MOE_ROW_TILE
    dx = d + MOE_META
    i32 = jnp.int32

    hb, rt, cnt = pl.pallas_call(
        functools.partial(_router_kernel, d=d, n_exp=n_exp, tm=tm),
        grid=(n_tiles,),
        in_specs=[st.tok_spec(d), st.mod_spec(mod), wr_pad.spec()],
        out_specs=[st.tok_spec(d),
                   pl.BlockSpec((None, ROUTE_ROWS, tm), lambda i: (i, 0, 0)),
                   pl.BlockSpec(memory_space=pltpu.SMEM)],
        out_shape=[jax.ShapeDtypeStruct((n_tok, d), BF16),
                   jax.ShapeDtypeStruct((n_tiles, ROUTE_ROWS, tm), F32),
                   jax.ShapeDtypeStruct((n_tiles, n_exp), i32)],
        compiler_params=_cparams("arbitrary"),
        name="moe_router",
    )(x, mod.arr, wr_pad.arr)

    padded = -(-cnt // MOE_PIECE) * MOE_PIECE
    per_e = jnp.sum(padded, axis=0)
    per_e_full = -(-per_e // tf) * tf
    region_end = jnp.cumsum(per_e_full)
    region_start = region_end - per_e_full
    base = region_start[None, :] + jnp.cumsum(padded, axis=0) - padded
    seg = jnp.concatenate([base, cnt], axis=1).astype(i32)
    n_row_tiles = -(-(TOP_K * n_tok + n_tiles * n_exp * (MOE_PIECE - 1) + n_exp * (tf - 1)) // tf)
    tile_end = region_end // tf
    n_act = tile_end[-1]
    zfill = jnp.concatenate([
        jnp.stack([region_start + per_e, (per_e_full - per_e) // MOE_PIECE], axis=1).reshape(-1),
        jnp.stack([n_act, jnp.asarray(n_row_tiles, i32)])]).astype(i32)
    t_idx = jnp.minimum(jnp.arange(n_row_tiles, dtype=i32), n_act - 1)
    t_e = jnp.sum(t_idx[:, None] >= tile_end[None, :], axis=1).astype(i32)
    upper = (jnp.arange(tm)[:, None] < jnp.arange(tm)[None, :]).astype(BF16)

    xs, meta = pl.pallas_call(
        functools.partial(_dispatch_kernel, d=d, n_exp=n_exp, tm=tm, ls=ls, n_tiles=n_tiles, tf=tf),
        grid_spec=pltpu.PrefetchScalarGridSpec(
            num_scalar_prefetch=2,
            grid=(n_tiles,),
            in_specs=[st.tok_spec(d),
                      pl.BlockSpec((None, ROUTE_ROWS, tm), lambda i, *_: (i, 0, 0)),
                      _resident((tm, tm))],
            out_specs=[pl.BlockSpec(memory_space=pl.ANY), st.tok_spec(LANES)],
            scratch_shapes=[pltpu.VMEM((2, ls, dx), BF16), pltpu.VMEM((MOE_PIECE, dx), BF16),
                            pltpu.VMEM((tf, dx), BF16), pltpu.SemaphoreType.DMA((3,))],
        ),
        out_shape=[jax.ShapeDtypeStruct((n_row_tiles * tf, dx), BF16),
                   jax.ShapeDtypeStruct((n_tok, LANES), F32)],
        compiler_params=_cparams("arbitrary"),
        name="moe_dispatch",
    )(seg, zfill, hb, rt, upper)

    ys = pl.pallas_call(
        functools.partial(_expert_kernel, d=d, blocks=_ff_blocks(ff)),
        grid_spec=pltpu.PrefetchScalarGridSpec(
            num_scalar_prefetch=2,
            grid=(n_row_tiles,),
            in_specs=[pl.BlockSpec((tf, dx), lambda i, te, na: (i, 0)),
                      pl.BlockSpec((None, None, d, ff), lambda i, te, na: (li, te[i], 0, 0)),
                      pl.BlockSpec((None, None, d, ff), lambda i, te, na: (li, te[i], 0, 0)),
                      pl.BlockSpec((None, None, ff, d), lambda i, te, na: (li, te[i], 0, 0))],
            out_specs=pl.BlockSpec((tf, d), lambda i, te, na: (i, 0)),
        ),
        out_shape=jax.ShapeDtypeStruct((n_row_tiles * tf, d), BF16),
        compiler_params=_cparams("arbitrary"),
        name="moe_experts",
    )(t_e, n_act.reshape(1).astype(i32), xs, wg.arr, wu.arr, wd.arr)

    return pl.pallas_call(
        functools.partial(_combine_kernel, d=d, alpha=alpha, n_exp=n_exp, tm=tm, ls=ls,
                          n_tiles=n_tiles),
        grid_spec=pltpu.PrefetchScalarGridSpec(
            num_scalar_prefetch=1,
            grid=(n_tiles,),
            in_specs=[st.tok_spec(d), st.mod_spec(mod), st.tok_spec(LANES),
                      pl.BlockSpec(memory_space=pl.ANY), lg.spec(), lb.spec()],
            out_specs=st.tok_spec(d),
            scratch_shapes=[pltpu.VMEM((2, ls, d), BF16), pltpu.SemaphoreType.DMA((2,))],
        ),
        out_shape=jax.ShapeDtypeStruct((n_tok, d), F32),
        compiler_params=_cparams("arbitrary"),
        name="moe_combine",
    )(seg, x, mod.arr, meta, ys, lg.arr, lb.arr)


def kernel(x, c, ctx, c_ctx, w_mod, b_mod, w_in, w_gk2, b_gk, gla_g, pool_w, pool_scale, sgu_ln_g, sgu_ln_b, sgu_w, sgu_b, w_gate, w_branch, w_o, ln_g, ln_b, w_ffn_gate, w_ffn_up, w_ffn_down, w_router, w_exp_gate, w_exp_up, w_exp_down):
    bsz, t, d = x.shape
    tc = ctx.shape[1]
    depth = w_mod.shape[0]
    n6 = w_mod.shape[2]
    alpha = (2 * depth) ** 0.25
    rows = t // GRID_W
    ng = len(POOL_WINDOWS)
    gk_rank = w_gk2.shape[2]
    assert d == 1024 and w_in.shape[2] == 1792 + 2 * gk_rank and 2 * gk_rank <= LANES
    assert t % GLA_C == 0 and tc % GLA_C == 0 and bsz % POOL_PACK == 0

    n_rows = -(-(bsz + 1) // SUBLANES) * SUBLANES
    cvec = jnp.zeros((n_rows, d), F32).at[:bsz].set(c).at[bsz].set(c_ctx)
    mod_all = _modulation(cvec, w_mod, b_mod)

    lat = _lat_stream(bsz, t)
    cst = _ctx_stream(bsz, tc)
    xl = x.reshape(bsz * t, d)
    xc = ctx.reshape(bsz * tc, d)

    pool_tab_l = _pool_tables(t, rows)
    pool_tab_c = _pool_tables(tc, None)
    chan_tabs = _fourier_chan_tables(ng)
    dft_l = tuple(a.astype(BF16) for a in _dft_tables(t))
    dft_c = tuple(a.astype(BF16) for a in _dft_tables(tc))

    gk0 = 768
    gk1 = gk0 + 2 * gk_rank

    mod4 = mod_all.reshape(depth, n_rows, 1, n6)
    win_all = jnp.concatenate(
        [w_in[:, :, :gk0], w_in[:, :, gk1:], w_in[:, :, gk0:gk1],
         jnp.zeros((depth, d, LANES - 2 * gk_rank), F32)], axis=2).astype(BF16)
    wgk2_all = jnp.zeros((depth, LANES, 256), F32)
    wgk2_all = wgk2_all.at[:, 0:gk_rank, 0:128].set(w_gk2[:, 0])
    wgk2_all = wgk2_all.at[:, gk_rank:2 * gk_rank, 128:256].set(w_gk2[:, 1]).astype(BF16)
    bgk_all = b_gk.reshape(depth, 1, 256)
    gla_g_all = gla_g.reshape(depth, 1, 256)
    grp = 256 // ng
    pw_bd_all = jnp.einsum("ab,lgcd->lgacbd", jnp.eye(POOL_PACK, dtype=F32), pool_w).reshape(
        depth, ng, 256, 256).astype(BF16)
    ps_all = jnp.tile(pool_scale.reshape(depth, ng, 1, grp), (1, 1, 1, POOL_PACK))
    sgu_lg_all = sgu_ln_g.reshape(depth, 1, 256)
    sgu_lb_all = sgu_ln_b.reshape(depth, 1, 256)
    sgu_w_all = sgu_w.astype(BF16)
    sgu_bm_all = jnp.repeat(jnp.swapaxes(sgu_b, 1, 2), grp, axis=2)
    wg_all = w_gate.astype(BF16)
    wb_all = w_branch.astype(BF16)
    wo_all = w_o.astype(BF16)
    ln_g4 = ln_g.reshape(depth, 2, 1, d)
    ln_b4 = ln_b.reshape(depth, 2, 1, d)
    ffn_all = (w_ffn_gate.astype(BF16), w_ffn_up.astype(BF16), w_ffn_down.astype(BF16))
    n_exp = w_router.shape[2]
    wr_all = jnp.zeros((w_router.shape[0], d, LANES), F32).at[:, :, :n_exp].set(w_router)
    exp_all = (w_exp_gate.astype(BF16), w_exp_up.astype(BF16), w_exp_down.astype(BF16))

    for l in range(depth):
        last = l == depth - 1
        mod = _Sel(mod4, l)
        win, wgk2, bgk = _Sel(win_all, l), _Sel(wgk2_all, l), _Sel(bgk_all, l)
        wg, wb, wo = _Sel(wg_all, l), _Sel(wb_all, l), _Sel(wo_all, l)
        lg0, lb0 = _Sel(ln_g4, l, 0), _Sel(ln_b4, l, 0)
        lg1, lb1 = _Sel(ln_g4, l, 1), _Sel(ln_b4, l, 1)

        hl, qk_l, v_l, r_l, p_l, f_l, s_l = _inproj(lat, xl, mod, win, wgk2, bgk)
        hc, qk_c, v_c, r_c, p_c, f_c, s_c = _inproj(cst, xc, mod, win, wgk2, bgk)
        ya_l, ya_c = _gla(bsz, t, tc, (qk_l, v_l, r_l), (qk_c, v_c, r_c), _Sel(gla_g_all, l))
        sg = (_Sel(sgu_lg_all, l), _Sel(sgu_lb_all, l), _Sel(sgu_w_all, l), _Sel(sgu_bm_all, l))
        yb_l = _pool(bsz, t, p_l, pool_tab_l, pw_bd_all, ps_all, l)
        yc_l = _fourier(bsz, t, f_l, chan_tabs, dft_l)
        yd_l = _sgu(lat, s_l, *sg)
        xl = _merge(lat, xl, hl, (ya_l, yb_l, yc_l, yd_l), mod, wg, wb, wo, lg0, lb0, alpha)
        if not last:
            yb_c = _pool(bsz, tc, p_c, pool_tab_c, pw_bd_all, ps_all, l)
            yc_c = _fourier(bsz, tc, f_c, chan_tabs, dft_c)
            yd_c = _sgu(cst, s_c, *sg)
            xc = _merge(cst, xc, hc, (ya_c, yb_c, yc_c, yd_c), mod, wg, wb, wo, lg0, lb0, alpha)

        i = l // 2
        if l % 2 == 0:
            fw = tuple(_Sel(w, i) for w in ffn_all)
            xl = _ffn(lat, xl, mod, *fw, lg1, lb1, alpha)
            if not last:
                xc = _ffn(cst, xc, mod, *fw, lg1, lb1, alpha)
        else:
            ew = tuple(_Sel(w, i) for w in exp_all)
            xl = _moe(lat, xl, mod, _Sel(wr_all, i), *ew, lg1, lb1, alpha)
            if not last:
                xc = _moe(cst, xc, mod, _Sel(wr_all, i), *ew, lg1, lb1, alpha)
    return xl.reshape(bsz, t, d)
```

```python
import functools
import math

import jax
import jax.numpy as jnp
from jax import lax
from jax.experimental import pallas as pl
from jax.experimental.pallas import tpu as pltpu

F32 = jnp.float32
BF16 = jnp.bfloat16

GRID_W = 64
GLA_HEADS = 4
GK_TAU = 16.0
POOL_WINDOWS = (2, 4, 8, 16)
SGU_CHUNK = 128
TOP_K = 2
EPS = 1e-6

LANES = 128
SUBLANES = 8
VMEM_LIMIT_BYTES = 56 * 1024 * 1024

GLA_C = 128
GLA_UNROLL = 4


def _cparams(*sem):
    return pltpu.CompilerParams(dimension_semantics=sem, vmem_limit_bytes=VMEM_LIMIT_BYTES)


def _resident(shape):
    nd = len(shape)
    return pl.BlockSpec(shape, lambda *_: (0,) * nd, pipeline_mode=pl.Buffered(1))


class _Sel:
    def __init__(self, arr, *idx):
        self.arr, self.idx = arr, tuple(idx)

    @property
    def shape(self):
        return tuple(self.arr.shape[len(self.idx):])

    def spec(self):
        idx, tail = self.idx, self.shape
        return pl.BlockSpec((None,) * len(idx) + tail, lambda *_: idx + (0,) * len(tail),
                            pipeline_mode=pl.Buffered(1))


def _dot(a, b):
    return jnp.dot(a, b, preferred_element_type=F32)


def _dot_nt(a, b):
    return lax.dot_general(a, b, (((1,), (1,)), ((), ())), preferred_element_type=F32)


def _dot_tn(a, b):
    return lax.dot_general(a, b, (((0,), (0,)), ((), ())), preferred_element_type=F32)


def _split3(x):
    hi = x.astype(BF16)
    r1 = x - hi.astype(F32)
    mid = r1.astype(BF16)
    lo = (r1 - mid.astype(F32)).astype(BF16)
    return hi, mid, lo


def _dot_exact_rhs(x, w):
    hi, mid, lo = _split3(x)
    return _dot(hi, w) + _dot(mid, w) + _dot(lo, w)


def _dot_exact_lhs(w, x):
    hi, mid, lo = _split3(x)
    return _dot(w, hi) + _dot(w, mid) + _dot(w, lo)


def _ln(x):
    mu = jnp.mean(x, axis=-1, keepdims=True)
    xc = x - mu
    var = jnp.mean(xc * xc, axis=-1, keepdims=True)
    return xc * lax.rsqrt(var + EPS)


def _sigmoid(x):
    return 1.0 / (1.0 + jnp.exp(-x))


def _silu(x):
    return x * _sigmoid(x)


def _gelu_tanh(x):
    return 0.5 * x * (1.0 + jnp.tanh(math.sqrt(2.0 / math.pi) * (x + 0.044715 * (x * x * x))))


def _pick_tile(n, candidates=(1024, 512, 256, 128)):
    for c in candidates:
        if n % c == 0:
            return c
    raise ValueError(f"no tile for {n}")


def _mod_kernel(c_ref, w_ref, b_ref, o_ref):
    s = _silu(c_ref[...])
    sh, sm, sl = _split3(s)
    w = w_ref[0]
    wh = w.astype(BF16)
    wl = (w - wh.astype(F32)).astype(BF16)
    acc = _dot(sh, wh) + (_dot(sh, wl) + _dot(sm, wh)) + (_dot(sm, wl) + _dot(sl, wh))
    o_ref[0] = acc + b_ref[0]


def _modulation(cvec, w_mod, b_mod):
    depth, d, n6 = w_mod.shape
    rows = cvec.shape[0]
    tn = _pick_tile(n6, (1536, 1024, 512, 256, 128))
    return pl.pallas_call(
        _mod_kernel,
        grid=(depth, n6 // tn),
        in_specs=[
            pl.BlockSpec((rows, d), lambda l, j: (0, 0)),
            pl.BlockSpec((1, d, tn), lambda l, j: (l, 0, j)),
            pl.BlockSpec((1, 1, tn), lambda l, j: (l, 0, j)),
        ],
        out_specs=pl.BlockSpec((1, rows, tn), lambda l, j: (l, 0, j)),
        out_shape=jax.ShapeDtypeStruct((depth, rows, n6), F32),
        compiler_params=_cparams("parallel", "parallel"),
        name="modulation",
    )(cvec, w_mod, b_mod.reshape(depth, 1, n6))


class _Stream:
    def __init__(self, n_tok, seq, mod_row_of_tile, tm):
        self.n_tok, self.seq, self.tm = n_tok, seq, tm
        self.mod_row_of_tile = mod_row_of_tile
        self.n_tiles = n_tok // tm

    def mod_spec(self, mod):
        f = self.mod_row_of_tile
        (l,) = mod.idx
        return pl.BlockSpec((None, 1, 1, mod.shape[-1]), lambda i, *_: (l, f(i), 0, 0))

    def tok_spec(self, width):
        return pl.BlockSpec((self.tm, width), lambda i, *_: (i, 0))


def _lat_stream(bsz, t):
    tm = _pick_tile(t, (512, 256, 128))
    per = t // tm
    return _Stream(bsz * t, t, lambda i: i // per, tm)


def _ctx_stream(bsz, tc):
    tm = _pick_tile(bsz * tc, (512, 256, 128))
    return _Stream(bsz * tc, tc, lambda i: bsz, tm)


def _inproj_kernel(x_ref, mod_ref, win_ref, wgk2_ref, bgk_ref,
                   h_ref, qkla_ref, v_ref, r_ref, pool_ref, fft_ref, sgu_ref, *, d, q_scale):
    shift = mod_ref[0, :, 0:d]
    scale = mod_ref[0, :, d:2 * d]
    c = GLA_C
    tri = jnp.where(lax.broadcasted_iota(jnp.int32, (c, c), 1)
                    <= lax.broadcasted_iota(jnp.int32, (c, c), 0), 1.0, 0.0).astype(BF16)
    x = x_ref[...]
    hb = (_ln(x) * (1.0 + scale) + shift).astype(BF16)
    h_ref[...] = hb
    p = _dot(hb, win_ref[...])
    qkla_ref[:, 0:128] = p[:, 0:128] * q_scale
    qkla_ref[:, 128:256] = p[:, 128:256]
    v_ref[...] = p[:, 256:512].astype(BF16)
    r_ref[...] = p[:, 512:768].astype(BF16)
    pool_ref[...] = p[:, 768:1024].astype(BF16)
    fft_ref[...] = p[:, 1024:1280].astype(BF16)
    sgu_ref[...] = p[:, 1280:1792].astype(BF16)
    z = _dot(p[:, 1792:1920].astype(BF16), wgk2_ref[...]) + bgk_ref[...]
    la = (jnp.minimum(z, 0.0) - jnp.log(1.0 + jnp.exp(-jnp.abs(z)))) * (1.0 / GK_TAU)
    split = jnp.concatenate(_split3(la), axis=-1)
    for ch in range(x.shape[0] // c):
        rows = slice(ch * c, (ch + 1) * c)
        pp = _dot(tri, split[rows, :])
        pref = (pp[:, 0:256] + pp[:, 256:512]) + pp[:, 512:768]
        qkla_ref[rows, 256:384] = pref[:, 0:128]
        pb = pref[:, 128:256]
        qkla_ref[rows, 384:512] = pb[c - 1:c, :] - pb + la[rows, 128:256]


def _inproj(st, x, mod, win, wgk2, bgk):
    d = x.shape[1]
    outs = [
        jax.ShapeDtypeStruct((st.n_tok, d), BF16),
        jax.ShapeDtypeStruct((st.n_tok, 512), F32),
        jax.ShapeDtypeStruct((st.n_tok, 256), BF16),
        jax.ShapeDtypeStruct((st.n_tok, 256), BF16),
        jax.ShapeDtypeStruct((st.n_tok, 256), BF16),
        jax.ShapeDtypeStruct((st.n_tok, 256), BF16),
        jax.ShapeDtypeStruct((st.n_tok, 512), BF16),
    ]
    dk = 128 // GLA_HEADS
    return pl.pallas_call(
        functools.partial(_inproj_kernel, d=d, q_scale=dk ** -0.5),
        grid=(st.n_tiles,),
        in_specs=[st.tok_spec(d), st.mod_spec(mod), win.spec(), wgk2.spec(), bgk.spec()],
        out_specs=[st.tok_spec(o.shape[1]) for o in outs],
        out_shape=outs,
        compiler_params=_cparams("parallel"),
        name="in_proj",
    )(x, mod.arr, win.arr, wgk2.arr, bgk.arr)


def _gla_kernel(qk_l, v_l, r_l, qk_c, v_c, r_c, g_ref, y_l, y_c, ht, oacc_l, oacc_c,
                *, nch_l, nch_c):
    c = GLA_C
    nh = GLA_HEADS
    ii4 = lax.broadcasted_iota(jnp.int32, (nh * c, c), 0) & (c - 1)
    jj4 = lax.broadcasted_iota(jnp.int32, (nh * c, c), 1)
    lane_qk = lax.broadcasted_iota(jnp.int32, (c, 128), 1) // (128 // nh)
    lane_v = lax.broadcasted_iota(jnp.int32, (c, 256), 1) // (256 // nh)
    st_mask = (lax.broadcasted_iota(jnp.int32, (256, 128), 0) // (256 // nh)
               == lax.broadcasted_iota(jnp.int32, (256, 128), 1) // (128 // nh))

    def chunk(qk, v, oacc, idx, fwd):
        dirn = 0 if fwd else 1
        pair4 = (jj4 <= ii4) if fwd else (jj4 > ii4)
        mid, last = (c // 2 - 1, c - 1) if fwd else (c // 2, 0)
        la0 = 256 if fwd else 384
        rows = pl.ds(pl.multiple_of(idx * c, c), c)
        q = qk[rows, 0:128]
        k = qk[rows, 128:256]
        b = qk[rows, la0:la0 + 128]
        b_mid = b[mid:mid + 1, :]
        b_last = b[last:last + 1, :]
        qe = q * jnp.exp(b - b_mid)
        ke = (k * jnp.exp(b_mid - b)).astype(BF16)
        kt = (k * jnp.exp(b_last - b)).astype(BF16)
        qa = (qe * jnp.exp(b_mid)).astype(BF16)
        qs = jnp.concatenate([jnp.where(lane_qk == h, qe, 0.0) for h in range(nh)],
                             axis=0).astype(BF16)
        att = jnp.where(pair4, _dot_nt(qs, ke), 0.0).astype(BF16)
        vv = v[rows, :]
        os_ = _dot(att, vv)
        o = _dot_nt(qa, ht[dirn].astype(BF16))
        for h in range(nh):
            o = o + jnp.where(lane_v == h, os_[h * c:(h + 1) * c, :], 0.0)
        oacc[dirn, rows, :] = o
        dht = _dot_tn(vv, kt)
        ht[dirn] = ht[dirn] * jnp.exp(b_last) + jnp.where(st_mask, dht, 0.0)

    def run(qk, v, oacc, nch):
        unroll = math.gcd(nch, GLA_UNROLL)

        def step(n, carry):
            for u in range(unroll):
                chunk(qk, v, oacc, n * unroll + u, True)
                chunk(qk, v, oacc, nch - 1 - (n * unroll + u), False)
            return carry

        lax.fori_loop(0, nch // unroll, step, 0)

    ht[...] = jnp.zeros_like(ht)
    run(qk_c, v_c, oacc_c, nch_c)
    run(qk_l, v_l, oacc_l, nch_l)

    hv = 256 // nh
    mh = jnp.where(lax.broadcasted_iota(jnp.int32, (256, 256), 0) // hv
                   == lax.broadcasted_iota(jnp.int32, (256, 256), 1) // hv, 1.0 / hv, 0.0).astype(BF16)
    g = g_ref[...]

    def finish(oacc, r, y, nch):
        unroll = math.gcd(nch, GLA_UNROLL)

        def step(n, carry):
            for u in range(unroll):
                rows = pl.ds(pl.multiple_of((n * unroll + u) * c, c), c)
                o = oacc[0, rows, :] + oacc[1, rows, :]
                sq = o * o
                sh = sq.astype(BF16)
                sl = (sq - sh.astype(F32)).astype(BF16)
                ms = _dot(sh, mh) + _dot(sl, mh)
                rr = r[rows, :].astype(F32)
                y[rows, :] = (o * lax.rsqrt(ms + EPS) * g * _silu(rr)).astype(BF16)
            return carry

        lax.fori_loop(0, nch // unroll, step, 0)

    finish(oacc_c, r_c, y_c, nch_c)
    finish(oacc_l, r_l, y_l, nch_l)


def _gla(bsz, t, tc, lat, ctx, gla_g):
    c = GLA_C
    nch_l, nch_c = t // c, tc // c

    def prep(arrs, seq):
        qkla, v, r = arrs
        return (qkla.reshape(bsz, seq, 512), v.reshape(bsz, seq, 256), r.reshape(bsz, seq, 256))

    def specs(seq):
        return [pl.BlockSpec((None, seq, 512), lambda b: (b, 0, 0)),
                pl.BlockSpec((None, seq, 256), lambda b: (b, 0, 0)),
                pl.BlockSpec((None, seq, 256), lambda b: (b, 0, 0))]

    y_l, y_c = pl.pallas_call(
        functools.partial(_gla_kernel, nch_l=nch_l, nch_c=nch_c),
        grid=(bsz,),
        in_specs=specs(t) + specs(tc) + [gla_g.spec()],
        out_specs=[pl.BlockSpec((None, t, 256), lambda b: (b, 0, 0)),
                   pl.BlockSpec((None, tc, 256), lambda b: (b, 0, 0))],
        out_shape=[jax.ShapeDtypeStruct((bsz, t, 256), BF16),
                   jax.ShapeDtypeStruct((bsz, tc, 256), BF16)],
        scratch_shapes=[pltpu.VMEM((2, 256, 128), F32), pltpu.VMEM((2, t, 256), F32),
                        pltpu.VMEM((2, tc, 256), F32)],
        compiler_params=_cparams("parallel"),
        name="gla",
    )(*prep(lat, t), *prep(ctx, tc), gla_g.arr)
    return y_l.reshape(bsz * t, 256), y_c.reshape(bsz * tc, 256)


POOL_PACK = 4


def _pool_kernel(x_ref, s_ref, ic_ref, pw_ref, ps_ref, o_ref, xp_ref, op_ref, *, ng, seq):
    g = pl.program_id(1)
    grp = 256 // ng
    rc = min(seq, 512)

    @pl.when(g == 0)
    def _():
        def pack(c, carry):
            rows = pl.ds(pl.multiple_of(c * rc, rc), rc)
            for gg in range(ng):
                xp_ref[gg, rows, :] = jnp.concatenate(
                    [x_ref[b, rows, gg * grp:(gg + 1) * grp] for b in range(POOL_PACK)], axis=-1)
            return carry

        lax.fori_loop(0, seq // rc, pack, 0)

    x = xp_ref[g]
    ic = ic_ref[...]
    m = _dot(s_ref[...], x) * jnp.concatenate([ic, ic], axis=-1)
    dlt = (m - x.astype(F32)).astype(BF16)
    op_ref[g] = (_dot(dlt, pw_ref[...]) * ps_ref[...]).astype(BF16)

    @pl.when(g == ng - 1)
    def _():
        def unpack(c, carry):
            rows = pl.ds(pl.multiple_of(c * rc, rc), rc)
            for b in range(POOL_PACK):
                o_ref[b, rows, :] = jnp.concatenate(
                    [op_ref[gg, rows, b * grp:(b + 1) * grp] for gg in range(ng)], axis=-1)
            return carry

        lax.fori_loop(0, seq // rc, unpack, 0)


def _window_bounds(pos, n, win):
    return jnp.clip(pos - win // 2, 0, n), jnp.clip(pos + (win - win // 2), 0, n)


def _pool_tables(seq, rows):
    mats, invs = [], []
    tok = jnp.arange(seq)
    for win in POOL_WINDOWS:
        if rows is None:
            lo, hi = _window_bounds(tok, seq, win)
            member = (tok[None, :] >= lo[:, None]) & (tok[None, :] < hi[:, None])
            cnt = hi - lo
        else:
            rr, cc = tok // GRID_W, tok % GRID_W
            lo_r, hi_r = _window_bounds(rr, rows, win)
            lo_c, hi_c = _window_bounds(cc, GRID_W, win)
            member = ((rr[None, :] >= lo_r[:, None]) & (rr[None, :] < hi_r[:, None])
                      & (cc[None, :] >= lo_c[:, None]) & (cc[None, :] < hi_c[:, None]))
            cnt = (hi_r - lo_r) * (hi_c - lo_c)
        mats.append(member.astype(BF16))
        invs.append(jnp.broadcast_to((1.0 / cnt.astype(F32))[:, None], (seq, LANES)))
    return jnp.stack(mats), jnp.stack(invs)


def _pool(bsz, seq, xp, tables, pw_bd, ps_t, l):
    smat, inv = tables
    ng = len(POOL_WINDOWS)
    nq = bsz // POOL_PACK
    blk = pl.BlockSpec((POOL_PACK, seq, 256), lambda q, g: (q, 0, 0))
    out = pl.pallas_call(
        functools.partial(_pool_kernel, ng=ng, seq=seq),
        grid=(nq, ng),
        in_specs=[blk,
                  pl.BlockSpec((None, seq, seq), lambda q, g: (g, 0, 0)),
                  pl.BlockSpec((None, seq, LANES), lambda q, g: (g, 0, 0)),
                  pl.BlockSpec((None, None, 256, 256), lambda q, g: (l, g, 0, 0)),
                  pl.BlockSpec((None, None, 1, 256), lambda q, g: (l, g, 0, 0))],
        out_specs=blk,
        out_shape=jax.ShapeDtypeStruct((bsz, seq, 256), BF16),
        scratch_shapes=[pltpu.VMEM((ng, seq, 256), BF16), pltpu.VMEM((ng, seq, 256), BF16)],
        compiler_params=_cparams("arbitrary", "arbitrary"),
        name="pool",
    )(xp.reshape(bsz, seq, 256), smat, inv, pw_bd, ps_t)
    return out.reshape(bsz * seq, 256)


def _fourier_kernel(z_ref, cb_ref, sb_ref, ct_ref, st_ref, o_ref, *, scale):
    z = z_ref[...]
    zc = _dot(z, cb_ref[...]).astype(BF16)
    zs = _dot(z, sb_ref[...]).astype(BF16)
    y = _dot(ct_ref[...], zc) - _dot(st_ref[...], zs)
    o_ref[...] = (y * scale).astype(BF16)


def _dft_tables(n):
    k = jnp.arange(n, dtype=jnp.int32)
    ang = ((k[:, None] * k[None, :]) % n).astype(F32) * (2.0 * math.pi / n)
    return jnp.cos(ang), jnp.sin(ang)


def _fourier_chan_tables(ng):
    grp = 256 // ng
    cg, sg = _dft_tables(grp)
    eye = jnp.eye(ng, dtype=F32)
    return jnp.kron(eye, cg).astype(BF16), jnp.kron(eye, sg).astype(BF16)


def _fourier(bsz, seq, z, chan_tabs, seq_tabs):
    cb, sb = chan_tabs
    ct, st = seq_tabs
    grp = cb.shape[0] // 4
    out = pl.pallas_call(
        functools.partial(_fourier_kernel, scale=1.0 / math.sqrt(seq * grp)),
        grid=(bsz,),
        in_specs=[pl.BlockSpec((None, seq, 256), lambda b: (b, 0, 0)),
                  _resident((256, 256)), _resident((256, 256)),
                  _resident((seq, seq)), _resident((seq, seq))],
        out_specs=pl.BlockSpec((None, seq, 256), lambda b: (b, 0, 0)),
        out_shape=jax.ShapeDtypeStruct((bsz, seq, 256), BF16),
        compiler_params=_cparams("parallel"),
        name="fourier",
    )(z.reshape(bsz, seq, 256), cb, sb, ct, st)
    return out.reshape(bsz * seq, 256)


def _sgu_kernel(s_ref, lg_ref, lb_ref, w_ref, bm_ref, o_ref, *, n_chunks, ng):
    u = _gelu_tanh(s_ref[:, 0:256].astype(F32))
    v = _gelu_tanh(s_ref[:, 256:512].astype(F32))
    vb = (_ln(v) * lg_ref[...] + lb_ref[...]).astype(BF16)
    lane_g = lax.broadcasted_iota(jnp.int32, (SGU_CHUNK, 256), 1) // (256 // ng)
    for ch in range(n_chunks):
        rows = slice(ch * SGU_CHUNK, (ch + 1) * SGU_CHUNK)
        vc = vb[rows, :]
        s = bm_ref[...]
        for g in range(ng):
            s = s + jnp.where(lane_g == g, _dot(w_ref[g], vc), 0.0)
        o_ref[rows, :] = (u[rows, :] * s).astype(BF16)


def _sgu(st, sgu_in, lg, lb, w, bm):
    ng = w.shape[0]
    tm = st.tm
    return pl.pallas_call(
        functools.partial(_sgu_kernel, n_chunks=tm // SGU_CHUNK, ng=ng),
        grid=(st.n_tiles,),
        in_specs=[st.tok_spec(512), lg.spec(), lb.spec(), w.spec(), bm.spec()],
        out_specs=st.tok_spec(256),
        out_shape=jax.ShapeDtypeStruct((st.n_tok, 256), BF16),
        compiler_params=_cparams("parallel"),
        name="sgu",
    )(sgu_in, lg.arr, lb.arr, w.arr, bm.arr)


MERGE_COLS = 512


def _merge_kernel(x_ref, h_ref, y0, y1, y2, y3, mod_ref, wg_ref, wb_ref, wo_ref, lg_ref, lb_ref,
                  o_ref, *, d, alpha):
    h = h_ref[...]
    ys = (y0[...], y1[...], y2[...], y3[...])
    parts = []
    for cb in range(d // MERGE_COLS):
        cols = slice(cb * MERGE_COLS, (cb + 1) * MERGE_COLS)
        m = None
        for i in range(4):
            t = _sigmoid(_dot(h, wg_ref[i, :, cols])) * _dot(ys[i], wb_ref[i, :, cols])
            m = t if m is None else m + t
        parts.append(m.astype(BF16))
    y = _dot(jnp.concatenate(parts, axis=-1), wo_ref[...])
    gate = mod_ref[0, :, 2 * d:3 * d]
    xn = alpha * x_ref[...] + gate * y
    o_ref[...] = _ln(xn) * lg_ref[...] + lb_ref[...]


def _merge(st, x, h, ys, mod, wg, wb, wo, lg, lb, alpha):
    d = x.shape[1]
    return pl.pallas_call(
        functools.partial(_merge_kernel, d=d, alpha=alpha),
        grid=(st.n_tiles,),
        in_specs=[st.tok_spec(d), st.tok_spec(d)] + [st.tok_spec(256)] * 4
        + [st.mod_spec(mod), wg.spec(), wb.spec(), wo.spec(), lg.spec(), lb.spec()],
        out_specs=st.tok_spec(d),
        out_shape=jax.ShapeDtypeStruct((st.n_tok, d), F32),
        compiler_params=_cparams("parallel"),
        name="merge",
    )(x, h, *ys, mod.arr, wg.arr, wb.arr, wo.arr, lg.arr, lb.arr)


def _ff_blocks(ff):
    if ff <= 1536 or ff % 256:
        return [(0, ff)]
    first = (ff // 512 + (1 if (ff // 256) % 2 else 0)) * 256
    return [(0, first), (first, ff)]


def _ffn_kernel(x_ref, mod_ref, wg_ref, wu_ref, wd_ref, lg_ref, lb_ref, o_ref, *, d, alpha, blocks):
    x = x_ref[...]
    shift = mod_ref[0, :, 3 * d:4 * d]
    scale = mod_ref[0, :, 4 * d:5 * d]
    gate = mod_ref[0, :, 5 * d:6 * d]
    hb = (_ln(x) * (1.0 + scale) + shift).astype(BF16)
    y = None
    for lo, hi in blocks:
        a = (_silu(_dot(hb, wg_ref[:, lo:hi])) * _dot(hb, wu_ref[:, lo:hi])).astype(BF16)
        t = _dot(a, wd_ref[lo:hi, :])
        y = t if y is None else y + t
    xn = alpha * x + gate * y
    o_ref[...] = _ln(xn) * lg_ref[...] + lb_ref[...]


def _ffn(st, x, mod, wg, wu, wd, lg, lb, alpha):
    d = x.shape[1]
    ff = wg.shape[1]
    return pl.pallas_call(
        functools.partial(_ffn_kernel, d=d, alpha=alpha, blocks=_ff_blocks(ff)),
        grid=(st.n_tiles,),
        in_specs=[st.tok_spec(d), st.mod_spec(mod), wg.spec(), wu.spec(), wd.spec(),
                  lg.spec(), lb.spec()],
        out_specs=st.tok_spec(d),
        out_shape=jax.ShapeDtypeStruct((st.n_tok, d), F32),
        compiler_params=_cparams("parallel"),
        name="ffn",
    )(x, mod.arr, wg.arr, wu.arr, wd.arr, lg.arr, lb.arr)


MOE_PIECE = 16
MOE_ROW_TILE = 512
MOE_META = LANES


def _moe_local_rows(tm, n_exp):
    return -(-(TOP_K * tm + n_exp * (MOE_PIECE - 1)) // LANES) * LANES


ROUTE_ROWS = SUBLANES


def _router_kernel(x_ref, mod_ref, wr_ref, hb_ref, rt_ref, cnt_ref, *, d, n_exp, tm):
    i = pl.program_id(0)
    x = x_ref[...]
    shift = mod_ref[0, :, 3 * d:4 * d]
    scale = mod_ref[0, :, 4 * d:5 * d]
    h = _ln(x) * (1.0 + scale) + shift
    hb = h.astype(BF16)
    hb_ref[...] = hb

    wr = wr_ref[...]
    wrh = wr.astype(BF16)
    wrl = (wr - wrh.astype(F32)).astype(BF16)
    hl = (h - hb.astype(F32)).astype(BF16)
    logits = _dot(hb, wrh) + (_dot(hb, wrl) + _dot(hl, wrh))
    lt = logits.T[0:n_exp, :]
    sub = lax.broadcasted_iota(jnp.int32, (n_exp, tm), 0).astype(F32)
    neg = jnp.float32(-jnp.inf)
    v1 = jnp.max(lt, axis=0, keepdims=True)
    i1 = jnp.min(jnp.where(lt == v1, sub, float(n_exp)), axis=0, keepdims=True)
    rest = jnp.where(sub == i1, neg, lt)
    v2 = jnp.max(rest, axis=0, keepdims=True)
    i2 = jnp.min(jnp.where(rest == v2, sub, float(n_exp)), axis=0, keepdims=True)
    e2 = jnp.exp(v2 - v1)
    w1 = 1.0 / (1.0 + e2)
    w2 = e2 / (1.0 + e2)
    oht = jnp.where(sub == i1, 1.0, jnp.where(sub == i2, 1.0, 0.0))
    cnt_col = jnp.sum(oht, axis=1, keepdims=True)
    for e in range(n_exp):
        cnt_ref[i, e] = cnt_col[e, 0].astype(jnp.int32)
    rec = lax.broadcasted_iota(jnp.int32, (ROUTE_ROWS, tm), 0)
    rt_ref[...] = jnp.where(rec == 0, i1, jnp.where(rec == 1, i2, jnp.where(
        rec == 2, w1, jnp.where(rec == 3, w2, 0.0))))


def _dispatch_kernel(seg_ref, zf_ref, hb_ref, rt_ref, u_ref, xs_hbm, meta_ref,
                     sorted_ref, zero_ref, ztile_ref, sem, *, d, n_exp, tm, ls, n_tiles, tf):
    i = pl.program_id(0)
    hb = hb_ref[...]
    i1 = rt_ref[0:1, :]
    i2 = rt_ref[1:2, :]
    w1 = rt_ref[2:3, :]
    w2 = rt_ref[3:4, :]
    sub = lax.broadcasted_iota(jnp.int32, (n_exp, tm), 0).astype(F32)
    oh1 = sub == i1
    oh2 = sub == i2
    oht = jnp.where(oh1, 1.0, jnp.where(oh2, 1.0, 0.0))
    cumt = _dot(oht.astype(BF16), u_ref[...])
    cnts = [seg_ref[i, n_exp + e] for e in range(n_exp)]
    npc = [(cn + (MOE_PIECE - 1)) // MOE_PIECE for cn in cnts]
    los = []
    lo = jnp.int32(0)
    for e in range(n_exp):
        los.append(lo)
        lo = lo + npc[e] * MOE_PIECE
    lo_b = jnp.zeros((n_exp, tm), F32)
    for e in range(n_exp):
        lo_b = jnp.where(sub == float(e), los[e].astype(F32), lo_b)
    lpt = cumt + lo_b
    lp1 = jnp.sum(jnp.where(oh1, lpt, 0.0), axis=0, keepdims=True)
    lp2 = jnp.sum(jnp.where(oh2, lpt, 0.0), axis=0, keepdims=True)

    row = lax.broadcasted_iota(jnp.int32, (ls, tm), 0).astype(F32)
    m1 = row == lp1
    m2 = row == lp2
    perm = jnp.where(m1, 1.0, jnp.where(m2, 1.0, 0.0)).astype(BF16)
    gcol = jnp.sum(jnp.where(m1, w1, jnp.where(m2, w2, 0.0)), axis=1, keepdims=True)
    slot = i % 2

    def piece(src_view, src_row, dst_row, s):
        return pltpu.make_async_copy(
            src_view.at[pl.ds(pl.multiple_of(src_row, MOE_PIECE), MOE_PIECE), :],
            xs_hbm.at[pl.ds(pl.multiple_of(dst_row, MOE_PIECE), MOE_PIECE), :], sem.at[s])

    def drain(tile):
        n = jnp.int32(0)
        for e in range(n_exp):
            n = n + (seg_ref[tile, n_exp + e] + (MOE_PIECE - 1)) // MOE_PIECE

        def wait(j, carry):
            piece(sorted_ref.at[0], 0, 0, tile % 2).wait()
            return carry

        lax.fori_loop(0, n, wait, 0)

    @pl.when(i >= 2)
    def _():
        drain(i - 2)

    sorted_ref[slot, :, 0:d] = _dot(perm, hb).astype(BF16)
    ghi = gcol.astype(BF16).astype(F32)
    r1 = gcol - ghi
    gmid = r1.astype(BF16).astype(F32)
    glo = r1 - gmid
    lane = lax.broadcasted_iota(jnp.int32, (ls, MOE_META), 1)
    sorted_ref[slot, :, d:d + MOE_META] = jnp.where(
        lane == 0, ghi, jnp.where(lane == 1, gmid, jnp.where(lane == 2, glo, 0.0))).astype(BF16)

    rows8 = jnp.where(sub == 0.0, lp1, jnp.where(sub == 1.0, lp2, 0.0))
    padded = jnp.concatenate([rows8, jnp.zeros((LANES - n_exp, tm), F32)], axis=0)
    meta_ref[...] = padded.T

    for e in range(n_exp):
        base = seg_ref[i, e]

        def start(j, carry, e=e, base=base):
            piece(sorted_ref.at[slot], los[e] + j * MOE_PIECE, base + j * MOE_PIECE, slot).start()
            return carry

        lax.fori_loop(0, npc[e], start, 0)

    @pl.when(i == n_tiles - 1)
    def _():
        @pl.when(i >= 1)
        def _():
            drain(i - 1)

        drain(i)
        zero_ref[...] = jnp.zeros_like(zero_ref)
        ztile_ref[...] = jnp.zeros_like(ztile_ref)
        n_zero = jnp.int32(0)
        for e in range(n_exp):
            z0 = zf_ref[2 * e]
            nz = zf_ref[2 * e + 1]

            def zstart(j, carry, z0=z0):
                piece(zero_ref, 0, z0 + j * MOE_PIECE, 2).start()
                return carry

            lax.fori_loop(0, nz, zstart, 0)
            n_zero = n_zero + nz

        def zwait(j, carry):
            piece(zero_ref, 0, 0, 2).wait()
            return carry

        lax.fori_loop(0, n_zero, zwait, 0)

        def tile_copy(t):
            return pltpu.make_async_copy(
                ztile_ref, xs_hbm.at[pl.ds(pl.multiple_of(t * tf, tf), tf), :], sem.at[2])

        t0 = zf_ref[2 * n_exp]
        t1 = zf_ref[2 * n_exp + 1]

        def tstart(t, carry):
            tile_copy(t).start()
            return carry

        def twait(t, carry):
            tile_copy(t).wait()
            return carry

        lax.fori_loop(t0, t1, tstart, 0)
        lax.fori_loop(t0, t1, twait, 0)


def _expert_kernel(te_ref, na_ref, xs_ref, wg_ref, wu_ref, wd_ref, ys_ref, *, d, blocks):
    active = pl.program_id(0) < na_ref[0]

    @pl.when(jnp.logical_not(active))
    def _():
        ys_ref[...] = jnp.zeros_like(ys_ref)

    @pl.when(active)
    def _():
        hb = xs_ref[:, 0:d]
        gate = jnp.sum(xs_ref[:, d:d + MOE_META].astype(F32), axis=-1, keepdims=True)
        y = None
        for lo, hi in blocks:
            a = (_silu(_dot(hb, wg_ref[:, lo:hi])) * _dot(hb, wu_ref[:, lo:hi])).astype(BF16)
            t = _dot(a, wd_ref[lo:hi, :])
            y = t if y is None else y + t
        ys_ref[...] = (y * gate).astype(BF16)


def _combine_kernel(seg_ref, x_ref, mod_ref, meta_ref, ys_hbm, lg_ref, lb_ref, o_ref, ysl_ref, sem,
                    *, d, alpha, n_exp, tm, ls, n_tiles):
    i = pl.program_id(0)

    def piece(src_row, dst_row, s):
        return pltpu.make_async_copy(
            ys_hbm.at[pl.ds(pl.multiple_of(src_row, MOE_PIECE), MOE_PIECE), :],
            ysl_ref.at[s, pl.ds(pl.multiple_of(dst_row, MOE_PIECE), MOE_PIECE), :], sem.at[s])

    def n_pieces(tile):
        n = jnp.int32(0)
        for e in range(n_exp):
            n = n + (seg_ref[tile, n_exp + e] + (MOE_PIECE - 1)) // MOE_PIECE
        return n

    def fetch(tile):
        s = tile % 2
        lo = jnp.int32(0)
        for e in range(n_exp):
            base = seg_ref[tile, e]
            npc = (seg_ref[tile, n_exp + e] + (MOE_PIECE - 1)) // MOE_PIECE

            def start(j, carry, base=base, lo=lo):
                piece(base + j * MOE_PIECE, lo + j * MOE_PIECE, s).start()
                return carry

            lax.fori_loop(0, npc, start, 0)
            lo = lo + npc * MOE_PIECE

        def clear(j, carry):
            ysl_ref[s, pl.ds(pl.multiple_of(j * MOE_PIECE, MOE_PIECE), MOE_PIECE), :] = jnp.zeros(
                (MOE_PIECE, d), BF16)
            return carry

        lax.fori_loop(lo // MOE_PIECE, ls // MOE_PIECE, clear, 0)

    @pl.when(i == 0)
    def _():
        fetch(i)

    @pl.when(i + 1 < n_tiles)
    def _():
        fetch(i + 1)

    slot = i % 2

    def wait(j, carry):
        piece(0, 0, slot).wait()
        return carry

    lax.fori_loop(0, n_pieces(i), wait, 0)

    lp1 = meta_ref[:, 0:1]
    lp2 = meta_ref[:, 1:2]
    col = lax.broadcasted_iota(jnp.int32, (tm, ls), 1).astype(F32)
    perm_t = jnp.where(col == lp1, 1.0, jnp.where(col == lp2, 1.0, 0.0)).astype(BF16)
    y = _dot(perm_t, ysl_ref[slot])
    gate = mod_ref[0, :, 5 * d:6 * d]
    xn = alpha * x_ref[...] + gate * y
    o_ref[...] = _ln(xn) * lg_ref[...] + lb_ref[...]


def _moe(st, x, mod, wr_pad, wg, wu, wd, lg, lb, alpha):
    d = x.shape[1]
    n_exp, _, ff = wg.shape
    (li,) = wg.idx
    tm, n_tiles, n_tok = st.tm, st.n_tiles, st.n_tok
    ls = _moe_local_rows(tm, n_exp)
    tf = MOE_ROW_TILE
    dx = d + MOE_META
    i32 = jnp.int32

    hb, rt, cnt = pl.pallas_call(
        functools.partial(_router_kernel, d=d, n_exp=n_exp, tm=tm),
        grid=(n_tiles,),
        in_specs=[st.tok_spec(d), st.mod_spec(mod), wr_pad.spec()],
        out_specs=[st.tok_spec(d),
                   pl.BlockSpec((None, ROUTE_ROWS, tm), lambda i: (i, 0, 0)),
                   pl.BlockSpec(memory_space=pltpu.SMEM)],
        out_shape=[jax.ShapeDtypeStruct((n_tok, d), BF16),
                   jax.ShapeDtypeStruct((n_tiles, ROUTE_ROWS, tm), F32),
                   jax.ShapeDtypeStruct((n_tiles, n_exp), i32)],
        compiler_params=_cparams("arbitrary"),
        name="moe_router",
    )(x, mod.arr, wr_pad.arr)

    padded = -(-cnt // MOE_PIECE) * MOE_PIECE
    per_e = jnp.sum(padded, axis=0)
    per_e_full = -(-per_e // tf) * tf
    region_end = jnp.cumsum(per_e_full)
    region_start = region_end - per_e_full
    base = region_start[None, :] + jnp.cumsum(padded, axis=0) - padded
    seg = jnp.concatenate([base, cnt], axis=1).astype(i32)
    n_row_tiles = -(-(TOP_K * n_tok + n_tiles * n_exp * (MOE_PIECE - 1) + n_exp * (tf - 1)) // tf)
    tile_end = region_end // tf
    n_act = tile_end[-1]
    zfill = jnp.concatenate([
        jnp.stack([region_start + per_e, (per_e_full - per_e) // MOE_PIECE], axis=1).reshape(-1),
        jnp.stack([n_act, jnp.asarray(n_row_tiles, i32)])]).astype(i32)
    t_idx = jnp.minimum(jnp.arange(n_row_tiles, dtype=i32), n_act - 1)
    t_e = jnp.sum(t_idx[:, None] >= tile_end[None, :], axis=1).astype(i32)
    upper = (jnp.arange(tm)[:, None] < jnp.arange(tm)[None, :]).astype(BF16)

    xs, meta = pl.pallas_call(
        functools.partial(_dispatch_kernel, d=d, n_exp=n_exp, tm=tm, ls=ls, n_tiles=n_tiles, tf=tf),
        grid_spec=pltpu.PrefetchScalarGridSpec(
            num_scalar_prefetch=2,
            grid=(n_tiles,),
            in_specs=[st.tok_spec(d),
                      pl.BlockSpec((None, ROUTE_ROWS, tm), lambda i, *_: (i, 0, 0)),
                      _resident((tm, tm))],
            out_specs=[pl.BlockSpec(memory_space=pl.ANY), st.tok_spec(LANES)],
            scratch_shapes=[pltpu.VMEM((2, ls, dx), BF16), pltpu.VMEM((MOE_PIECE, dx), BF16),
                            pltpu.VMEM((tf, dx), BF16), pltpu.SemaphoreType.DMA((3,))],
        ),
        out_shape=[jax.ShapeDtypeStruct((n_row_tiles * tf, dx), BF16),
                   jax.ShapeDtypeStruct((n_tok, LANES), F32)],
        compiler_params=_cparams("arbitrary"),
        name="moe_dispatch",
    )(seg, zfill, hb, rt, upper)

    ys = pl.pallas_call(
        functools.partial(_expert_kernel, d=d, blocks=_ff_blocks(ff)),
        grid_spec=pltpu.PrefetchScalarGridSpec(
            num_scalar_prefetch=2,
            grid=(n_row_tiles,),
            in_specs=[pl.BlockSpec((tf, dx), lambda i, te, na: (i, 0)),
                      pl.BlockSpec((None, None, d, ff), lambda i, te, na: (li, te[i], 0, 0)),
                      pl.BlockSpec((None, None, d, ff), lambda i, te, na: (li, te[i], 0, 0)),
                      pl.BlockSpec((None, None, ff, d), lambda i, te, na: (li, te[i], 0, 0))],
            out_specs=pl.BlockSpec((tf, d), lambda i, te, na: (i, 0)),
        ),
        out_shape=jax.ShapeDtypeStruct((n_row_tiles * tf, d), BF16),
        compiler_params=_cparams("arbitrary"),
        name="moe_experts",
    )(t_e, n_act.reshape(1).astype(i32), xs, wg.arr, wu.arr, wd.arr)

    return pl.pallas_call(
        functools.partial(_combine_kernel, d=d, alpha=alpha, n_exp=n_exp, tm=tm, ls=ls,
                          n_tiles=n_tiles),
        grid_spec=pltpu.PrefetchScalarGridSpec(
            num_scalar_prefetch=1,
            grid=(n_tiles,),
            in_specs=[st.tok_spec(d), st.mod_spec(mod), st.tok_spec(LANES),
                      pl.BlockSpec(memory_space=pl.ANY), lg.spec(), lb.spec()],
            out_specs=st.tok_spec(d),
            scratch_shapes=[pltpu.VMEM((2, ls, d), BF16), pltpu.SemaphoreType.DMA((2,))],
        ),
        out_shape=jax.ShapeDtypeStruct((n_tok, d), F32),
        compiler_params=_cparams("arbitrary"),
        name="moe_combine",
    )(seg, x, mod.arr, meta, ys, lg.arr, lb.arr)


def kernel(x, c, ctx, c_ctx, w_mod, b_mod, w_in, w_gk2, b_gk, gla_g, pool_w, pool_scale, sgu_ln_g, sgu_ln_b, sgu_w, sgu_b, w_gate, w_branch, w_o, ln_g, ln_b, w_ffn_gate, w_ffn_up, w_ffn_down, w_router, w_exp_gate, w_exp_up, w_exp_down):
    bsz, t, d = x.shape
    tc = ctx.shape[1]
    depth = w_mod.shape[0]
    n6 = w_mod.shape[2]
    alpha = (2 * depth) ** 0.25
    rows = t // GRID_W
    ng = len(POOL_WINDOWS)
    gk_rank = w_gk2.shape[2]
    assert d == 1024 and w_in.shape[2] == 1792 + 2 * gk_rank and 2 * gk_rank <= LANES
    assert t % GLA_C == 0 and tc % GLA_C == 0 and bsz % POOL_PACK == 0

    n_rows = -(-(bsz + 1) // SUBLANES) * SUBLANES
    cvec = jnp.zeros((n_rows, d), F32).at[:bsz].set(c).at[bsz].set(c_ctx)
    mod_all = _modulation(cvec, w_mod, b_mod)

    lat = _lat_stream(bsz, t)
    cst = _ctx_stream(bsz, tc)
    xl = x.reshape(bsz * t, d)
    xc = ctx.reshape(bsz * tc, d)

    pool_tab_l = _pool_tables(t, rows)
    pool_tab_c = _pool_tables(tc, None)
    chan_tabs = _fourier_chan_tables(ng)
    dft_l = tuple(a.astype(BF16) for a in _dft_tables(t))
    dft_c = tuple(a.astype(BF16) for a in _dft_tables(tc))

    gk0 = 768
    gk1 = gk0 + 2 * gk_rank

    mod4 = mod_all.reshape(depth, n_rows, 1, n6)
    win_all = jnp.concatenate(
        [w_in[:, :, :gk0], w_in[:, :, gk1:], w_in[:, :, gk0:gk1],
         jnp.zeros((depth, d, LANES - 2 * gk_rank), F32)], axis=2).astype(BF16)
    wgk2_all = jnp.zeros((depth, LANES, 256), F32)
    wgk2_all = wgk2_all.at[:, 0:gk_rank, 0:128].set(w_gk2[:, 0])
    wgk2_all = wgk2_all.at[:, gk_rank:2 * gk_rank, 128:256].set(w_gk2[:, 1]).astype(BF16)
    bgk_all = b_gk.reshape(depth, 1, 256)
    gla_g_all = gla_g.reshape(depth, 1, 256)
    grp = 256 // ng
    pw_bd_all = jnp.einsum("ab,lgcd->lgacbd", jnp.eye(POOL_PACK, dtype=F32), pool_w).reshape(
        depth, ng, 256, 256).astype(BF16)
    ps_all = jnp.tile(pool_scale.reshape(depth, ng, 1, grp), (1, 1, 1, POOL_PACK))
    sgu_lg_all = sgu_ln_g.reshape(depth, 1, 256)
    sgu_lb_all = sgu_ln_b.reshape(depth, 1, 256)
    sgu_w_all = sgu_w.astype(BF16)
    sgu_bm_all = jnp.repeat(jnp.swapaxes(sgu_b, 1, 2), grp, axis=2)
    wg_all = w_gate.astype(BF16)
    wb_all = w_branch.astype(BF16)
    wo_all = w_o.astype(BF16)
    ln_g4 = ln_g.reshape(depth, 2, 1, d)
    ln_b4 = ln_b.reshape(depth, 2, 1, d)
    ffn_all = (w_ffn_gate.astype(BF16), w_ffn_up.astype(BF16), w_ffn_down.astype(BF16))
    n_exp = w_router.shape[2]
    wr_all = jnp.zeros((w_router.shape[0], d, LANES), F32).at[:, :, :n_exp].set(w_router)
    exp_all = (w_exp_gate.astype(BF16), w_exp_up.astype(BF16), w_exp_down.astype(BF16))

    for l in range(depth):
        last = l == depth - 1
        mod = _Sel(mod4, l)
        win, wgk2, bgk = _Sel(win_all, l), _Sel(wgk2_all, l), _Sel(bgk_all, l)
        wg, wb, wo = _Sel(wg_all, l), _Sel(wb_all, l), _Sel(wo_all, l)
        lg0, lb0 = _Sel(ln_g4, l, 0), _Sel(ln_b4, l, 0)
        lg1, lb1 = _Sel(ln_g4, l, 1), _Sel(ln_b4, l, 1)

        hl, qk_l, v_l, r_l, p_l, f_l, s_l = _inproj(lat, xl, mod, win, wgk2, bgk)
        hc, qk_c, v_c, r_c, p_c, f_c, s_c = _inproj(cst, xc, mod, win, wgk2, bgk)
        ya_l, ya_c = _gla(bsz, t, tc, (qk_l, v_l, r_l), (qk_c, v_c, r_c), _Sel(gla_g_all, l))
        sg = (_Sel(sgu_lg_all, l), _Sel(sgu_lb_all, l), _Sel(sgu_w_all, l), _Sel(sgu_bm_all, l))
        yb_l = _pool(bsz, t, p_l, pool_tab_l, pw_bd_all, ps_all, l)
        yc_l = _fourier(bsz, t, f_l, chan_tabs, dft_l)
        yd_l = _sgu(lat, s_l, *sg)
        xl = _merge(lat, xl, hl, (ya_l, yb_l, yc_l, yd_l), mod, wg, wb, wo, lg0, lb0, alpha)
        if not last:
            yb_c = _pool(bsz, tc, p_c, pool_tab_c, pw_bd_all, ps_all, l)
            yc_c = _fourier(bsz, tc, f_c, chan_tabs, dft_c)
            yd_c = _sgu(cst, s_c, *sg)
            xc = _merge(cst, xc, hc, (ya_c, yb_c, yc_c, yd_c), mod, wg, wb, wo, lg0, lb0, alpha)

        i = l // 2
        if l % 2 == 0:
            fw = tuple(_Sel(w, i) for w in ffn_all)
            xl = _ffn(lat, xl, mod, *fw, lg1, lb1, alpha)
            if not last:
                xc = _ffn(cst, xc, mod, *fw, lg1, lb1, alpha)
        else:
            ew = tuple(_Sel(w, i) for w in exp_all)
            xl = _moe(lat, xl, mod, _Sel(wr_all, i), *ew, lg1, lb1, alpha)
            if not last:
                xc = _moe(cst, xc, mod, _Sel(wr_all, i), *ew, lg1, lb1, alpha)
    return xl.reshape(bsz, t, d)
```

```python
import functools
import math

import jax
import jax.numpy as jnp
from jax import lax
from jax.experimental import pallas as pl
from jax.experimental.pallas import tpu as pltpu

F32 = jnp.float32
BF16 = jnp.bfloat16

GRID_W = 64
GLA_HEADS = 4
GK_TAU = 16.0
POOL_WINDOWS = (2, 4, 8, 16)
SGU_CHUNK = 128
TOP_K = 2
EPS = 1e-6

LANES = 128
SUBLANES = 8
VMEM_LIMIT_BYTES = 56 * 1024 * 1024

GLA_C = 128
GLA_UNROLL = 8


def _cparams(*sem):
    return pltpu.CompilerParams(dimension_semantics=sem, vmem_limit_bytes=VMEM_LIMIT_BYTES)


def _resident(shape):
    nd = len(shape)
    return pl.BlockSpec(shape, lambda *_: (0,) * nd, pipeline_mode=pl.Buffered(1))


class _Sel:
    def __init__(self, arr, *idx):
        self.arr, self.idx = arr, tuple(idx)

    @property
    def shape(self):
        return tuple(self.arr.shape[len(self.idx):])

    def spec(self):
        idx, tail = self.idx, self.shape
        return pl.BlockSpec((None,) * len(idx) + tail, lambda *_: idx + (0,) * len(tail),
                            pipeline_mode=pl.Buffered(1))


def _dot(a, b):
    return jnp.dot(a, b, preferred_element_type=F32)


def _dot_nt(a, b):
    return lax.dot_general(a, b, (((1,), (1,)), ((), ())), preferred_element_type=F32)


def _dot_tn(a, b):
    return lax.dot_general(a, b, (((0,), (0,)), ((), ())), preferred_element_type=F32)


def _split3(x):
    hi = x.astype(BF16)
    r1 = x - hi.astype(F32)
    mid = r1.astype(BF16)
    lo = (r1 - mid.astype(F32)).astype(BF16)
    return hi, mid, lo


def _dot_exact_rhs(x, w):
    hi, mid, lo = _split3(x)
    return _dot(hi, w) + _dot(mid, w) + _dot(lo, w)


def _dot_exact_lhs(w, x):
    hi, mid, lo = _split3(x)
    return _dot(w, hi) + _dot(w, mid) + _dot(w, lo)


def _ln(x):
    mu = jnp.mean(x, axis=-1, keepdims=True)
    xc = x - mu
    var = jnp.mean(xc * xc, axis=-1, keepdims=True)
    return xc * lax.rsqrt(var + EPS)


def _sigmoid(x):
    return 1.0 / (1.0 + jnp.exp(-x))


def _silu(x):
    return x * _sigmoid(x)


def _gelu_tanh(x):
    return 0.5 * x * (1.0 + jnp.tanh(math.sqrt(2.0 / math.pi) * (x + 0.044715 * (x * x * x))))


def _pick_tile(n, candidates=(1024, 512, 256, 128)):
    for c in candidates:
        if n % c == 0:
            return c
    raise ValueError(f"no tile for {n}")


def _mod_kernel(c_ref, w_ref, b_ref, o_ref):
    s = _silu(c_ref[...])
    sh, sm, sl = _split3(s)
    w = w_ref[0]
    wh = w.astype(BF16)
    wl = (w - wh.astype(F32)).astype(BF16)
    acc = _dot(sh, wh) + (_dot(sh, wl) + _dot(sm, wh)) + (_dot(sm, wl) + _dot(sl, wh))
    o_ref[0] = acc + b_ref[0]


def _modulation(cvec, w_mod, b_mod):
    depth, d, n6 = w_mod.shape
    rows = cvec.shape[0]
    tn = _pick_tile(n6, (1536, 1024, 512, 256, 128))
    return pl.pallas_call(
        _mod_kernel,
        grid=(depth, n6 // tn),
        in_specs=[
            pl.BlockSpec((rows, d), lambda l, j: (0, 0)),
            pl.BlockSpec((1, d, tn), lambda l, j: (l, 0, j)),
            pl.BlockSpec((1, 1, tn), lambda l, j: (l, 0, j)),
        ],
        out_specs=pl.BlockSpec((1, rows, tn), lambda l, j: (l, 0, j)),
        out_shape=jax.ShapeDtypeStruct((depth, rows, n6), F32),
        compiler_params=_cparams("parallel", "parallel"),
        name="modulation",
    )(cvec, w_mod, b_mod.reshape(depth, 1, n6))


class _Stream:
    def __init__(self, n_tok, seq, mod_row_of_tile, tm):
        self.n_tok, self.seq, self.tm = n_tok, seq, tm
        self.mod_row_of_tile = mod_row_of_tile
        self.n_tiles = n_tok // tm

    def mod_spec(self, mod):
        f = self.mod_row_of_tile
        (l,) = mod.idx
        return pl.BlockSpec((None, 1, 1, mod.shape[-1]), lambda i, *_: (l, f(i), 0, 0))

    def tok_spec(self, width):
        return pl.BlockSpec((self.tm, width), lambda i, *_: (i, 0))


def _lat_stream(bsz, t):
    tm = _pick_tile(t, (512, 256, 128))
    per = t // tm
    return _Stream(bsz * t, t, lambda i: i // per, tm)


def _ctx_stream(bsz, tc):
    tm = _pick_tile(bsz * tc, (512, 256, 128))
    return _Stream(bsz * tc, tc, lambda i: bsz, tm)


def _inproj_kernel(x_ref, mod_ref, win_ref, wgk2_ref, bgk_ref,
                   h_ref, qkla_ref, v_ref, r_ref, pool_ref, fft_ref, sgu_ref, *, d, q_scale):
    shift = mod_ref[0, :, 0:d]
    scale = mod_ref[0, :, d:2 * d]
    c = GLA_C
    tri = jnp.where(lax.broadcasted_iota(jnp.int32, (c, c), 1)
                    <= lax.broadcasted_iota(jnp.int32, (c, c), 0), 1.0, 0.0).astype(BF16)
    x = x_ref[...]
    hb = (_ln(x) * (1.0 + scale) + shift).astype(BF16)
    h_ref[...] = hb
    p = _dot(hb, win_ref[...])
    qkla_ref[:, 0:128] = p[:, 0:128] * q_scale
    qkla_ref[:, 128:256] = p[:, 128:256]
    v_ref[...] = p[:, 256:512].astype(BF16)
    r_ref[...] = p[:, 512:768].astype(BF16)
    pool_ref[...] = p[:, 768:1024].astype(BF16)
    fft_ref[...] = p[:, 1024:1280].astype(BF16)
    sgu_ref[...] = p[:, 1280:1792].astype(BF16)
    z = _dot(p[:, 1792:1920].astype(BF16), wgk2_ref[...]) + bgk_ref[...]
    la = (jnp.minimum(z, 0.0) - jnp.log(1.0 + jnp.exp(-jnp.abs(z)))) * (1.0 / GK_TAU)
    split = jnp.concatenate(_split3(la), axis=-1)
    for ch in range(x.shape[0] // c):
        rows = slice(ch * c, (ch + 1) * c)
        pp = _dot(tri, split[rows, :])
        pref = (pp[:, 0:256] + pp[:, 256:512]) + pp[:, 512:768]
        qkla_ref[rows, 256:384] = pref[:, 0:128]
        pb = pref[:, 128:256]
        qkla_ref[rows, 384:512] = pb[c - 1:c, :] - pb + la[rows, 128:256]


def _inproj(st, x, mod, win, wgk2, bgk):
    d = x.shape[1]
    outs = [
        jax.ShapeDtypeStruct((st.n_tok, d), BF16),
        jax.ShapeDtypeStruct((st.n_tok, 512), F32),
        jax.ShapeDtypeStruct((st.n_tok, 256), BF16),
        jax.ShapeDtypeStruct((st.n_tok, 256), BF16),
        jax.ShapeDtypeStruct((st.n_tok, 256), BF16),
        jax.ShapeDtypeStruct((st.n_tok, 256), BF16),
        jax.ShapeDtypeStruct((st.n_tok, 512), BF16),
    ]
    dk = 128 // GLA_HEADS
    return pl.pallas_call(
        functools.partial(_inproj_kernel, d=d, q_scale=dk ** -0.5),
        grid=(st.n_tiles,),
        in_specs=[st.tok_spec(d), st.mod_spec(mod), win.spec(), wgk2.spec(), bgk.spec()],
        out_specs=[st.tok_spec(o.shape[1]) for o in outs],
        out_shape=outs,
        compiler_params=_cparams("parallel"),
        name="in_proj",
    )(x, mod.arr, win.arr, wgk2.arr, bgk.arr)


def _gla_kernel(qk_l, v_l, r_l, qk_c, v_c, r_c, g_ref, y_l, y_c, ht, oacc_l, oacc_c,
                *, nch_l, nch_c):
    c = GLA_C
    nh = GLA_HEADS
    ii4 = lax.broadcasted_iota(jnp.int32, (nh * c, c), 0) & (c - 1)
    jj4 = lax.broadcasted_iota(jnp.int32, (nh * c, c), 1)
    lane_qk = lax.broadcasted_iota(jnp.int32, (c, 128), 1) // (128 // nh)
    lane_v = lax.broadcasted_iota(jnp.int32, (c, 256), 1) // (256 // nh)
    st_mask = (lax.broadcasted_iota(jnp.int32, (256, 128), 0) // (256 // nh)
               == lax.broadcasted_iota(jnp.int32, (256, 128), 1) // (128 // nh))

    def chunk(qk, v, oacc, idx, fwd):
        dirn = 0 if fwd else 1
        pair4 = (jj4 <= ii4) if fwd else (jj4 > ii4)
        mid, last = (c // 2 - 1, c - 1) if fwd else (c // 2, 0)
        la0 = 256 if fwd else 384
        rows = pl.ds(pl.multiple_of(idx * c, c), c)
        q = qk[rows, 0:128]
        k = qk[rows, 128:256]
        b = qk[rows, la0:la0 + 128]
        b_mid = b[mid:mid + 1, :]
        b_last = b[last:last + 1, :]
        qe = q * jnp.exp(b - b_mid)
        ke = (k * jnp.exp(b_mid - b)).astype(BF16)
        kt = (k * jnp.exp(b_last - b)).astype(BF16)
        qa = (qe * jnp.exp(b_mid)).astype(BF16)
        qs = jnp.concatenate([jnp.where(lane_qk == h, qe, 0.0) for h in range(nh)],
                             axis=0).astype(BF16)
        att = jnp.where(pair4, _dot_nt(qs, ke), 0.0).astype(BF16)
        vv = v[rows, :]
        os_ = _dot(att, vv)
        o = _dot_nt(qa, ht[dirn].astype(BF16))
        for h in range(nh):
            o = o + jnp.where(lane_v == h, os_[h * c:(h + 1) * c, :], 0.0)
        oacc[dirn, rows, :] = o
        dht = _dot_tn(vv, kt)
        ht[dirn] = ht[dirn] * jnp.exp(b_last) + jnp.where(st_mask, dht, 0.0)

    def run(qk, v, oacc, nch):
        unroll = math.gcd(nch, GLA_UNROLL)

        def step(n, carry):
            for u in range(unroll):
                chunk(qk, v, oacc, n * unroll + u, True)
                chunk(qk, v, oacc, nch - 1 - (n * unroll + u), False)
            return carry

        lax.fori_loop(0, nch // unroll, step, 0)

    ht[...] = jnp.zeros_like(ht)
    run(qk_c, v_c, oacc_c, nch_c)
    run(qk_l, v_l, oacc_l, nch_l)

    hv = 256 // nh
    mh = jnp.where(lax.broadcasted_iota(jnp.int32, (256, 256), 0) // hv
                   == lax.broadcasted_iota(jnp.int32, (256, 256), 1) // hv, 1.0 / hv, 0.0).astype(BF16)
    g = g_ref[...]

    def finish(oacc, r, y, nch):
        unroll = math.gcd(nch, GLA_UNROLL)

        def step(n, carry):
            for u in range(unroll):
                rows = pl.ds(pl.multiple_of((n * unroll + u) * c, c), c)
                o = oacc[0, rows, :] + oacc[1, rows, :]
                sq = o * o
                sh = sq.astype(BF16)
                sl = (sq - sh.astype(F32)).astype(BF16)
                ms = _dot(sh, mh) + _dot(sl, mh)
                rr = r[rows, :].astype(F32)
                y[rows, :] = (o * lax.rsqrt(ms + EPS) * g * _silu(rr)).astype(BF16)
            return carry

        lax.fori_loop(0, nch // unroll, step, 0)

    finish(oacc_c, r_c, y_c, nch_c)
    finish(oacc_l, r_l, y_l, nch_l)


def _gla(bsz, t, tc, lat, ctx, gla_g):
    c = GLA_C
    nch_l, nch_c = t // c, tc // c

    def prep(arrs, seq):
        qkla, v, r = arrs
        return (qkla.reshape(bsz, seq, 512), v.reshape(bsz, seq, 256), r.reshape(bsz, seq, 256))

    def specs(seq):
        return [pl.BlockSpec((None, seq, 512), lambda b: (b, 0, 0)),
                pl.BlockSpec((None, seq, 256), lambda b: (b, 0, 0)),
                pl.BlockSpec((None, seq, 256), lambda b: (b, 0, 0))]

    y_l, y_c = pl.pallas_call(
        functools.partial(_gla_kernel, nch_l=nch_l, nch_c=nch_c),
        grid=(bsz,),
        in_specs=specs(t) + specs(tc) + [gla_g.spec()],
        out_specs=[pl.BlockSpec((None, t, 256), lambda b: (b, 0, 0)),
                   pl.BlockSpec((None, tc, 256), lambda b: (b, 0, 0))],
        out_shape=[jax.ShapeDtypeStruct((bsz, t, 256), BF16),
                   jax.ShapeDtypeStruct((bsz, tc, 256), BF16)],
        scratch_shapes=[pltpu.VMEM((2, 256, 128), F32), pltpu.VMEM((2, t, 256), F32),
                        pltpu.VMEM((2, tc, 256), F32)],
        compiler_params=_cparams("parallel"),
        name="gla",
    )(*prep(lat, t), *prep(ctx, tc), gla_g.arr)
    return y_l.reshape(bsz * t, 256), y_c.reshape(bsz * tc, 256)


POOL_PACK = 4
POOL_BAND_TILE = 256


def _pool_kernel(x_ref, s_ref, ic_ref, pw_ref, ps_ref, o_ref, xp_ref, op_ref, *, ng, seq, band):
    g = pl.program_id(1)
    grp = 256 // ng
    rc = min(seq, 512)

    @pl.when(g == 0)
    def _():
        def pack(c, carry):
            rows = pl.ds(pl.multiple_of(c * rc, rc), rc)
            for gg in range(ng):
                xp_ref[gg, rows, :] = jnp.concatenate(
                    [x_ref[b, rows, gg * grp:(gg + 1) * grp] for b in range(POOL_PACK)], axis=-1)
            return carry

        lax.fori_loop(0, seq // rc, pack, 0)

    bt = POOL_BAND_TILE if seq % POOL_BAND_TILE == 0 else seq
    nt = seq // bt
    for gk in range(ng):
        @pl.when(g == gk)
        def _(gk=gk):
            for i in range(nt):
                k0 = max(0, i - band[gk]) * bt
                k1 = min(nt, i + band[gk] + 1) * bt
                rows = slice(i * bt, (i + 1) * bt)
                ic = ic_ref[rows, :]
                m = _dot(s_ref[rows, k0:k1], xp_ref[gk, k0:k1, :]) * jnp.concatenate([ic, ic], axis=-1)
                dlt = (m - xp_ref[gk, rows, :].astype(F32)).astype(BF16)
                op_ref[gk, rows, :] = (_dot(dlt, pw_ref[...]) * ps_ref[...]).astype(BF16)

    @pl.when(g == ng - 1)
    def _():
        def unpack(c, carry):
            rows = pl.ds(pl.multiple_of(c * rc, rc), rc)
            for b in range(POOL_PACK):
                o_ref[b, rows, :] = jnp.concatenate(
                    [op_ref[gg, rows, b * grp:(b + 1) * grp] for gg in range(ng)], axis=-1)
            return carry

        lax.fori_loop(0, seq // rc, unpack, 0)


def _window_bounds(pos, n, win):
    return jnp.clip(pos - win // 2, 0, n), jnp.clip(pos + (win - win // 2), 0, n)


def _pool_tables(seq, rows):
    mats, invs = [], []
    tok = jnp.arange(seq)
    for win in POOL_WINDOWS:
        if rows is None:
            lo, hi = _window_bounds(tok, seq, win)
            member = (tok[None, :] >= lo[:, None]) & (tok[None, :] < hi[:, None])
            cnt = hi - lo
        else:
            rr, cc = tok // GRID_W, tok % GRID_W
            lo_r, hi_r = _window_bounds(rr, rows, win)
            lo_c, hi_c = _window_bounds(cc, GRID_W, win)
            member = ((rr[None, :] >= lo_r[:, None]) & (rr[None, :] < hi_r[:, None])
                      & (cc[None, :] >= lo_c[:, None]) & (cc[None, :] < hi_c[:, None]))
            cnt = (hi_r - lo_r) * (hi_c - lo_c)
        mats.append(member.astype(BF16))
        invs.append(jnp.broadcast_to((1.0 / cnt.astype(F32))[:, None], (seq, LANES)))
    return jnp.stack(mats), jnp.stack(invs)


def _pool(bsz, seq, xp, tables, pw_bd, ps_t, l, grid_w):
    smat, inv = tables
    ng = len(POOL_WINDOWS)
    nq = bsz // POOL_PACK
    reach = [(win - win // 2) * grid_w + (grid_w - 1 if grid_w > 1 else 0) for win in POOL_WINDOWS]
    band = tuple(-(-r // POOL_BAND_TILE) for r in reach)
    blk = pl.BlockSpec((POOL_PACK, seq, 256), lambda q, g: (q, 0, 0))
    out = pl.pallas_call(
        functools.partial(_pool_kernel, ng=ng, seq=seq, band=band),
        grid=(nq, ng),
        in_specs=[blk,
                  pl.BlockSpec((None, seq, seq), lambda q, g: (g, 0, 0)),
                  pl.BlockSpec((None, seq, LANES), lambda q, g: (g, 0, 0)),
                  pl.BlockSpec((None, None, 256, 256), lambda q, g: (l, g, 0, 0)),
                  pl.BlockSpec((None, None, 1, 256), lambda q, g: (l, g, 0, 0))],
        out_specs=blk,
        out_shape=jax.ShapeDtypeStruct((bsz, seq, 256), BF16),
        scratch_shapes=[pltpu.VMEM((ng, seq, 256), BF16), pltpu.VMEM((ng, seq, 256), BF16)],
        compiler_params=_cparams("arbitrary", "arbitrary"),
        name="pool",
    )(xp.reshape(bsz, seq, 256), smat, inv, pw_bd, ps_t)
    return out.reshape(bsz * seq, 256)


def _fourier_kernel(z_ref, cb_ref, sb_ref, ct_ref, st_ref, o_ref, *, scale):
    z = z_ref[...]
    zc = _dot(z, cb_ref[...]).astype(BF16)
    zs = _dot(z, sb_ref[...]).astype(BF16)
    y = _dot(ct_ref[...], zc) - _dot(st_ref[...], zs)
    o_ref[...] = (y * scale).astype(BF16)


def _dft_tables(n):
    k = jnp.arange(n, dtype=jnp.int32)
    ang = ((k[:, None] * k[None, :]) % n).astype(F32) * (2.0 * math.pi / n)
    return jnp.cos(ang), jnp.sin(ang)


def _fourier_chan_tables(ng):
    grp = 256 // ng
    cg, sg = _dft_tables(grp)
    eye = jnp.eye(ng, dtype=F32)
    return jnp.kron(eye, cg).astype(BF16), jnp.kron(eye, sg).astype(BF16)


def _fourier(bsz, seq, z, chan_tabs, seq_tabs):
    cb, sb = chan_tabs
    ct, st = seq_tabs
    grp = cb.shape[0] // 4
    out = pl.pallas_call(
        functools.partial(_fourier_kernel, scale=1.0 / math.sqrt(seq * grp)),
        grid=(bsz,),
        in_specs=[pl.BlockSpec((None, seq, 256), lambda b: (b, 0, 0)),
                  _resident((256, 256)), _resident((256, 256)),
                  _resident((seq, seq)), _resident((seq, seq))],
        out_specs=pl.BlockSpec((None, seq, 256), lambda b: (b, 0, 0)),
        out_shape=jax.ShapeDtypeStruct((bsz, seq, 256), BF16),
        compiler_params=_cparams("parallel"),
        name="fourier",
    )(z.reshape(bsz, seq, 256), cb, sb, ct, st)
    return out.reshape(bsz * seq, 256)


def _sgu_kernel(s_ref, lg_ref, lb_ref, w_ref, bm_ref, o_ref, *, n_chunks, ng):
    u = _gelu_tanh(s_ref[:, 0:256].astype(F32))
    v = _gelu_tanh(s_ref[:, 256:512].astype(F32))
    vb = (_ln(v) * lg_ref[...] + lb_ref[...]).astype(BF16)
    lane_g = lax.broadcasted_iota(jnp.int32, (SGU_CHUNK, 256), 1) // (256 // ng)
    for ch in range(n_chunks):
        rows = slice(ch * SGU_CHUNK, (ch + 1) * SGU_CHUNK)
        vc = vb[rows, :]
        s = bm_ref[...]
        for g in range(ng):
            s = s + jnp.where(lane_g == g, _dot(w_ref[g], vc), 0.0)
        o_ref[rows, :] = (u[rows, :] * s).astype(BF16)


def _sgu(st, sgu_in, lg, lb, w, bm):
    ng = w.shape[0]
    tm = st.tm
    return pl.pallas_call(
        functools.partial(_sgu_kernel, n_chunks=tm // SGU_CHUNK, ng=ng),
        grid=(st.n_tiles,),
        in_specs=[st.tok_spec(512), lg.spec(), lb.spec(), w.spec(), bm.spec()],
        out_specs=st.tok_spec(256),
        out_shape=jax.ShapeDtypeStruct((st.n_tok, 256), BF16),
        compiler_params=_cparams("parallel"),
        name="sgu",
    )(sgu_in, lg.arr, lb.arr, w.arr, bm.arr)


MERGE_COLS = 512


def _merge_kernel(x_ref, h_ref, y0, y1, y2, y3, mod_ref, wg_ref, wb_ref, wo_ref, lg_ref, lb_ref,
                  o_ref, *, d, alpha):
    h = h_ref[...]
    ys = (y0[...], y1[...], y2[...], y3[...])
    parts = []
    for cb in range(d // MERGE_COLS):
        cols = slice(cb * MERGE_COLS, (cb + 1) * MERGE_COLS)
        m = None
        for i in range(4):
            t = _sigmoid(_dot(h, wg_ref[i, :, cols])) * _dot(ys[i], wb_ref[i, :, cols])
            m = t if m is None else m + t
        parts.append(m.astype(BF16))
    y = _dot(jnp.concatenate(parts, axis=-1), wo_ref[...])
    gate = mod_ref[0, :, 2 * d:3 * d]
    xn = alpha * x_ref[...] + gate * y
    o_ref[...] = _ln(xn) * lg_ref[...] + lb_ref[...]


def _merge(st, x, h, ys, mod, wg, wb, wo, lg, lb, alpha):
    d = x.shape[1]
    return pl.pallas_call(
        functools.partial(_merge_kernel, d=d, alpha=alpha),
        grid=(st.n_tiles,),
        in_specs=[st.tok_spec(d), st.tok_spec(d)] + [st.tok_spec(256)] * 4
        + [st.mod_spec(mod), wg.spec(), wb.spec(), wo.spec(), lg.spec(), lb.spec()],
        out_specs=st.tok_spec(d),
        out_shape=jax.ShapeDtypeStruct((st.n_tok, d), F32),
        compiler_params=_cparams("parallel"),
        name="merge",
    )(x, h, *ys, mod.arr, wg.arr, wb.arr, wo.arr, lg.arr, lb.arr)


def _ff_blocks(ff):
    if ff <= 1536 or ff % 256:
        return [(0, ff)]
    first = (ff // 512 + (1 if (ff // 256) % 2 else 0)) * 256
    return [(0, first), (first, ff)]


def _ffn_kernel(x_ref, mod_ref, wg_ref, wu_ref, wd_ref, lg_ref, lb_ref, o_ref, *, d, alpha, blocks):
    x = x_ref[...]
    shift = mod_ref[0, :, 3 * d:4 * d]
    scale = mod_ref[0, :, 4 * d:5 * d]
    gate = mod_ref[0, :, 5 * d:6 * d]
    hb = (_ln(x) * (1.0 + scale) + shift).astype(BF16)
    y = None
    for lo, hi in blocks:
        a = (_silu(_dot(hb, wg_ref[:, lo:hi])) * _dot(hb, wu_ref[:, lo:hi])).astype(BF16)
        t = _dot(a, wd_ref[lo:hi, :])
        y = t if y is None else y + t
    xn = alpha * x + gate * y
    o_ref[...] = _ln(xn) * lg_ref[...] + lb_ref[...]


def _ffn(st, x, mod, wg, wu, wd, lg, lb, alpha):
    d = x.shape[1]
    ff = wg.shape[1]
    return pl.pallas_call(
        functools.partial(_ffn_kernel, d=d, alpha=alpha, blocks=_ff_blocks(ff)),
        grid=(st.n_tiles,),
        in_specs=[st.tok_spec(d), st.mod_spec(mod), wg.spec(), wu.spec(), wd.spec(),
                  lg.spec(), lb.spec()],
        out_specs=st.tok_spec(d),
        out_shape=jax.ShapeDtypeStruct((st.n_tok, d), F32),
        compiler_params=_cparams("parallel"),
        name="ffn",
    )(x, mod.arr, wg.arr, wu.arr, wd.arr, lg.arr, lb.arr)


MOE_PIECE = 16
MOE_ROW_TILE = 512
MOE_META = LANES


def _moe_local_rows(tm, n_exp):
    return -(-(TOP_K * tm + n_exp * (MOE_PIECE - 1)) // LANES) * LANES


ROUTE_ROWS = SUBLANES


def _router_kernel(x_ref, mod_ref, wr_ref, hb_ref, rt_ref, cnt_ref, *, d, n_exp, tm):
    i = pl.program_id(0)
    x = x_ref[...]
    shift = mod_ref[0, :, 3 * d:4 * d]
    scale = mod_ref[0, :, 4 * d:5 * d]
    h = _ln(x) * (1.0 + scale) + shift
    hb = h.astype(BF16)
    hb_ref[...] = hb

    wr = wr_ref[...]
    wrh = wr.astype(BF16)
    wrl = (wr - wrh.astype(F32)).astype(BF16)
    hl = (h - hb.astype(F32)).astype(BF16)
    logits = _dot(hb, wrh) + (_dot(hb, wrl) + _dot(hl, wrh))
    lt = logits.T[0:n_exp, :]
    sub = lax.broadcasted_iota(jnp.int32, (n_exp, tm), 0).astype(F32)
    neg = jnp.float32(-jnp.inf)
    v1 = jnp.max(lt, axis=0, keepdims=True)
    i1 = jnp.min(jnp.where(lt == v1, sub, float(n_exp)), axis=0, keepdims=True)
    rest = jnp.where(sub == i1, neg, lt)
    v2 = jnp.max(rest, axis=0, keepdims=True)
    i2 = jnp.min(jnp.where(rest == v2, sub, float(n_exp)), axis=0, keepdims=True)
    e2 = jnp.exp(v2 - v1)
    w1 = 1.0 / (1.0 + e2)
    w2 = e2 / (1.0 + e2)
    oht = jnp.where(sub == i1, 1.0, jnp.where(sub == i2, 1.0, 0.0))
    cnt_col = jnp.sum(oht, axis=1, keepdims=True)
    for e in range(n_exp):
        cnt_ref[i, e] = cnt_col[e, 0].astype(jnp.int32)
    rec = lax.broadcasted_iota(jnp.int32, (ROUTE_ROWS, tm), 0)
    rt_ref[...] = jnp.where(rec == 0, i1, jnp.where(rec == 1, i2, jnp.where(
        rec == 2, w1, jnp.where(rec == 3, w2, 0.0))))


def _dispatch_kernel(seg_ref, zf_ref, hb_ref, rt_ref, u_ref, xs_hbm, meta_ref,
                     sorted_ref, zero_ref, ztile_ref, sem, *, d, n_exp, tm, ls, n_tiles, tf):
    i = pl.program_id(0)
    hb = hb_ref[...]
    i1 = rt_ref[0:1, :]
    i2 = rt_ref[1:2, :]
    w1 = rt_ref[2:3, :]
    w2 = rt_ref[3:4, :]
    sub = lax.broadcasted_iota(jnp.int32, (n_exp, tm), 0).astype(F32)
    oh1 = sub == i1
    oh2 = sub == i2
    oht = jnp.where(oh1, 1.0, jnp.where(oh2, 1.0, 0.0))
    cumt = _dot(oht.astype(BF16), u_ref[...])
    cnts = [seg_ref[i, n_exp + e] for e in range(n_exp)]
    npc = [(cn + (MOE_PIECE - 1)) // MOE_PIECE for cn in cnts]
    los = []
    lo = jnp.int32(0)
    for e in range(n_exp):
        los.append(lo)
        lo = lo + npc[e] * MOE_PIECE
    lo_b = jnp.zeros((n_exp, tm), F32)
    for e in range(n_exp):
        lo_b = jnp.where(sub == float(e), los[e].astype(F32), lo_b)
    lpt = cumt + lo_b
    lp1 = jnp.sum(jnp.where(oh1, lpt, 0.0), axis=0, keepdims=True)
    lp2 = jnp.sum(jnp.where(oh2, lpt, 0.0), axis=0, keepdims=True)

    row = lax.broadcasted_iota(jnp.int32, (ls, tm), 0).astype(F32)
    m1 = row == lp1
    m2 = row == lp2
    perm = jnp.where(m1, 1.0, jnp.where(m2, 1.0, 0.0)).astype(BF16)
    gcol = jnp.sum(jnp.where(m1, w1, jnp.where(m2, w2, 0.0)), axis=1, keepdims=True)
    ghi = gcol.astype(BF16).astype(F32)
    r1 = gcol - ghi
    gmid = r1.astype(BF16).astype(F32)
    glo = r1 - gmid
    lane = lax.broadcasted_iota(jnp.int32, (ls, MOE_META), 1)
    gate_lanes = jnp.where(
        lane == 0, ghi, jnp.where(lane == 1, gmid, jnp.where(lane == 2, glo, 0.0))).astype(BF16)

    def piece(src_ref, src_row, dst_row, s):
        return pltpu.make_async_copy(
            src_ref.at[pl.ds(pl.multiple_of(src_row, MOE_PIECE), MOE_PIECE), :],
            xs_hbm.at[pl.ds(pl.multiple_of(dst_row, MOE_PIECE), MOE_PIECE), :], sem.at[s])

    def drain(tile):
        n = jnp.int32(0)
        for e in range(n_exp):
            n = n + (seg_ref[tile, n_exp + e] + (MOE_PIECE - 1)) // MOE_PIECE

        def wait(j, carry):
            piece(sorted_ref, 0, 0, 0).wait()
            return carry

        lax.fori_loop(0, n, wait, 0)

    @pl.when(i >= 1)
    def _():
        drain(i - 1)

    sorted_ref[:, 0:d] = _dot(perm, hb).astype(BF16)
    sorted_ref[:, d:d + MOE_META] = gate_lanes

    rows8 = jnp.where(sub == 0.0, lp1, jnp.where(sub == 1.0, lp2, 0.0))
    padded = jnp.concatenate([rows8, jnp.zeros((LANES - n_exp, tm), F32)], axis=0)
    meta_ref[...] = padded.T

    for e in range(n_exp):
        base = seg_ref[i, e]

        def start(j, carry, e=e, base=base):
            piece(sorted_ref, los[e] + j * MOE_PIECE, base + j * MOE_PIECE, 0).start()
            return carry

        lax.fori_loop(0, npc[e], start, 0)

    @pl.when(i == n_tiles - 1)
    def _():
        drain(i)
        zero_ref[...] = jnp.zeros_like(zero_ref)
        ztile_ref[...] = jnp.zeros_like(ztile_ref)
        n_zero = jnp.int32(0)
        for e in range(n_exp):
            z0 = zf_ref[2 * e]
            nz = zf_ref[2 * e + 1]

            def zstart(j, carry, z0=z0):
                piece(zero_ref, 0, z0 + j * MOE_PIECE, 1).start()
                return carry

            lax.fori_loop(0, nz, zstart, 0)
            n_zero = n_zero + nz

        def zwait(j, carry):
            piece(zero_ref, 0, 0, 1).wait()
            return carry

        lax.fori_loop(0, n_zero, zwait, 0)

        def tile_copy(t):
            return pltpu.make_async_copy(
                ztile_ref, xs_hbm.at[pl.ds(pl.multiple_of(t * tf, tf), tf), :], sem.at[1])

        t0 = zf_ref[2 * n_exp]
        t1 = zf_ref[2 * n_exp + 1]

        def tstart(t, carry):
            tile_copy(t).start()
            return carry

        def twait(t, carry):
            tile_copy(t).wait()
            return carry

        lax.fori_loop(t0, t1, tstart, 0)
        lax.fori_loop(t0, t1, twait, 0)


def _expert_kernel(te_ref, na_ref, xs_ref, wg_ref, wu_ref, wd_ref, ys_ref, *, d, blocks):
    active = pl.program_id(0) < na_ref[0]

    @pl.when(jnp.logical_not(active))
    def _():
        ys_ref[...] = jnp.zeros_like(ys_ref)

    @pl.when(active)
    def _():
        hb = xs_ref[:, 0:d]
        gate = jnp.sum(xs_ref[:, d:d + MOE_META].astype(F32), axis=-1, keepdims=True)
        y = None
        for lo, hi in blocks:
            a = (_silu(_dot(hb, wg_ref[:, lo:hi])) * _dot(hb, wu_ref[:, lo:hi])).astype(BF16)
            t = _dot(a, wd_ref[lo:hi, :])
            y = t if y is None else y + t
        ys_ref[...] = (y * gate).astype(BF16)


def _combine_kernel(seg_ref, x_ref, mod_ref, meta_ref, ys_hbm, lg_ref, lb_ref, o_ref, ysl_ref, sem,
                    *, d, alpha, n_exp, tm, ls, n_tiles):
    i = pl.program_id(0)

    def piece(src_row, dst_row, s):
        return pltpu.make_async_copy(
            ys_hbm.at[pl.ds(pl.multiple_of(src_row, MOE_PIECE), MOE_PIECE), :],
            ysl_ref.at[s, pl.ds(pl.multiple_of(dst_row, MOE_PIECE), MOE_PIECE), :], sem.at[s])

    def n_pieces(tile):
        n = jnp.int32(0)
        for e in range(n_exp):
            n = n + (seg_ref[tile, n_exp + e] + (MOE_PIECE - 1)) // MOE_PIECE
        return n

    def fetch(tile):
        s = tile % 2
        lo = jnp.int32(0)
        for e in range(n_exp):
            base = seg_ref[tile, e]
            npc = (seg_ref[tile, n_exp + e] + (MOE_PIECE - 1)) // MOE_PIECE

            def start(j, carry, base=base, lo=lo):
                piece(base + j * MOE_PIECE, lo + j * MOE_PIECE, s).start()
                return carry

            lax.fori_loop(0, npc, start, 0)
            lo = lo + npc * MOE_PIECE

        def clear(j, carry):
            ysl_ref[s, pl.ds(pl.multiple_of(j * MOE_PIECE, MOE_PIECE), MOE_PIECE), :] = jnp.zeros(
                (MOE_PIECE, d), BF16)
            return carry

        lax.fori_loop(lo // MOE_PIECE, ls // MOE_PIECE, clear, 0)

    @pl.when(i == 0)
    def _():
        fetch(i)

    @pl.when(i + 1 < n_tiles)
    def _():
        fetch(i + 1)

    slot = i % 2

    def wait(j, carry):
        piece(0, 0, slot).wait()
        return carry

    lax.fori_loop(0, n_pieces(i), wait, 0)

    lp1 = meta_ref[:, 0:1]
    lp2 = meta_ref[:, 1:2]
    col = lax.broadcasted_iota(jnp.int32, (tm, ls), 1).astype(F32)
    perm_t = jnp.where(col == lp1, 1.0, jnp.where(col == lp2, 1.0, 0.0)).astype(BF16)
    y = _dot(perm_t, ysl_ref[slot])
    gate = mod_ref[0, :, 5 * d:6 * d]
    xn = alpha * x_ref[...] + gate * y
    o_ref[...] = _ln(xn) * lg_ref[...] + lb_ref[...]


def _moe(st, x, mod, wr_pad, wg, wu, wd, lg, lb, alpha):
    d = x.shape[1]
    n_exp, _, ff = wg.shape
    (li,) = wg.idx
    tm, n_tiles, n_tok = st.tm, st.n_tiles, st.n_tok
    ls = _moe_local_rows(tm, n_exp)
    tf = MOE_ROW_TILE
    dx = d + MOE_META
    i32 = jnp.int32

    hb, rt, cnt = pl.pallas_call(
        functools.partial(_router_kernel, d=d, n_exp=n_exp, tm=tm),
        grid=(n_tiles,),
        in_specs=[st.tok_spec(d), st.mod_spec(mod), wr_pad.spec()],
        out_specs=[st.tok_spec(d),
                   pl.BlockSpec((None, ROUTE_ROWS, tm), lambda i: (i, 0, 0)),
                   pl.BlockSpec(memory_space=pltpu.SMEM)],
        out_shape=[jax.ShapeDtypeStruct((n_tok, d), BF16),
                   jax.ShapeDtypeStruct((n_tiles, ROUTE_ROWS, tm), F32),
                   jax.ShapeDtypeStruct((n_tiles, n_exp), i32)],
        compiler_params=_cparams("arbitrary"),
        name="moe_router",
    )(x, mod.arr, wr_pad.arr)

    padded = -(-cnt // MOE_PIECE) * MOE_PIECE
    per_e = jnp.sum(padded, axis=0)
    per_e_full = -(-per_e // tf) * tf
    region_end = jnp.cumsum(per_e_full)
    region_start = region_end - per_e_full
    base = region_start[None, :] + jnp.cumsum(padded, axis=0) - padded
    seg = jnp.concatenate([base, cnt], axis=1).astype(i32)
    n_row_tiles = -(-(TOP_K * n_tok + n_tiles * n_exp * (MOE_PIECE - 1) + n_exp * (tf - 1)) // tf)
    tile_end = region_end // tf
    n_act = tile_end[-1]
    zfill = jnp.concatenate([
        jnp.stack([region_start + per_e, (per_e_full - per_e) // MOE_PIECE], axis=1).reshape(-1),
        jnp.stack([n_act, jnp.asarray(n_row_tiles, i32)])]).astype(i32)
    t_idx = jnp.minimum(jnp.arange(n_row_tiles, dtype=i32), n_act - 1)
    t_e = jnp.sum(t_idx[:, None] >= tile_end[None, :], axis=1).astype(i32)
    upper = (jnp.arange(tm)[:, None] < jnp.arange(tm)[None, :]).astype(BF16)

    xs, meta = pl.pallas_call(
        functools.partial(_dispatch_kernel, d=d, n_exp=n_exp, tm=tm, ls=ls, n_tiles=n_tiles, tf=tf),
        grid_spec=pltpu.PrefetchScalarGridSpec(
            num_scalar_prefetch=2,
            grid=(n_tiles,),
            in_specs=[st.tok_spec(d),
                      pl.BlockSpec((None, ROUTE_ROWS, tm), lambda i, *_: (i, 0, 0)),
                      _resident((tm, tm))],
            out_specs=[pl.BlockSpec(memory_space=pl.ANY), st.tok_spec(LANES)],
            scratch_shapes=[pltpu.VMEM((ls, dx), BF16), pltpu.VMEM((MOE_PIECE, dx), BF16),
                            pltpu.VMEM((tf, dx), BF16), pltpu.SemaphoreType.DMA((2,))],
        ),
        out_shape=[jax.ShapeDtypeStruct((n_row_tiles * tf, dx), BF16),
                   jax.ShapeDtypeStruct((n_tok, LANES), F32)],
        compiler_params=_cparams("arbitrary"),
        name="moe_dispatch",
    )(seg, zfill, hb, rt, upper)

    ys = pl.pallas_call(
        functools.partial(_expert_kernel, d=d, blocks=_ff_blocks(ff)),
        grid_spec=pltpu.PrefetchScalarGridSpec(
            num_scalar_prefetch=2,
            grid=(n_row_tiles,),
            in_specs=[pl.BlockSpec((tf, dx), lambda i, te, na: (i, 0)),
                      pl.BlockSpec((None, None, d, ff), lambda i, te, na: (li, te[i], 0, 0)),
                      pl.BlockSpec((None, None, d, ff), lambda i, te, na: (li, te[i], 0, 0)),
                      pl.BlockSpec((None, None, ff, d), lambda i, te, na: (li, te[i], 0, 0))],
            out_specs=pl.BlockSpec((tf, d), lambda i, te, na: (i, 0)),
        ),
        out_shape=jax.ShapeDtypeStruct((n_row_tiles * tf, d), BF16),
        compiler_params=_cparams("arbitrary"),
        name="moe_experts",
    )(t_e, n_act.reshape(1).astype(i32), xs, wg.arr, wu.arr, wd.arr)

    return pl.pallas_call(
        functools.partial(_combine_kernel, d=d, alpha=alpha, n_exp=n_exp, tm=tm, ls=ls,
                          n_tiles=n_tiles),
        grid_spec=pltpu.PrefetchScalarGridSpec(
            num_scalar_prefetch=1,
            grid=(n_tiles,),
            in_specs=[st.tok_spec(d), st.mod_spec(mod), st.tok_spec(LANES),
                      pl.BlockSpec(memory_space=pl.ANY), lg.spec(), lb.spec()],
            out_specs=st.tok_spec(d),
            scratch_shapes=[pltpu.VMEM((2, ls, d), BF16), pltpu.SemaphoreType.DMA((2,))],
        ),
        out_shape=jax.ShapeDtypeStruct((n_tok, d), F32),
        compiler_params=_cparams("arbitrary"),
        name="moe_combine",
    )(seg, x, mod.arr, meta, ys, lg.arr, lb.arr)


def kernel(x, c, ctx, c_ctx, w_mod, b_mod, w_in, w_gk2, b_gk, gla_g, pool_w, pool_scale, sgu_ln_g, sgu_ln_b, sgu_w, sgu_b, w_gate, w_branch, w_o, ln_g, ln_b, w_ffn_gate, w_ffn_up, w_ffn_down, w_router, w_exp_gate, w_exp_up, w_exp_down):
    bsz, t, d = x.shape
    tc = ctx.shape[1]
    depth = w_mod.shape[0]
    n6 = w_mod.shape[2]
    alpha = (2 * depth) ** 0.25
    rows = t // GRID_W
    ng = len(POOL_WINDOWS)
    gk_rank = w_gk2.shape[2]
    assert d == 1024 and w_in.shape[2] == 1792 + 2 * gk_rank and 2 * gk_rank <= LANES
    assert t % GLA_C == 0 and tc % GLA_C == 0 and bsz % POOL_PACK == 0

    n_rows = -(-(bsz + 1) // SUBLANES) * SUBLANES
    cvec = jnp.zeros((n_rows, d), F32).at[:bsz].set(c).at[bsz].set(c_ctx)
    mod_all = _modulation(cvec, w_mod, b_mod)

    lat = _lat_stream(bsz, t)
    cst = _ctx_stream(bsz, tc)
    xl = x.reshape(bsz * t, d)
    xc = ctx.reshape(bsz * tc, d)

    pool_tab_l = _pool_tables(t, rows)
    pool_tab_c = _pool_tables(tc, None)
    chan_tabs = _fourier_chan_tables(ng)
    dft_l = tuple(a.astype(BF16) for a in _dft_tables(t))
    dft_c = tuple(a.astype(BF16) for a in _dft_tables(tc))

    gk0 = 768
    gk1 = gk0 + 2 * gk_rank

    mod4 = mod_all.reshape(depth, n_rows, 1, n6)
    win_all = jnp.concatenate(
        [w_in[:, :, :gk0], w_in[:, :, gk1:], w_in[:, :, gk0:gk1],
         jnp.zeros((depth, d, LANES - 2 * gk_rank), F32)], axis=2).astype(BF16)
    wgk2_all = jnp.zeros((depth, LANES, 256), F32)
    wgk2_all = wgk2_all.at[:, 0:gk_rank, 0:128].set(w_gk2[:, 0])
    wgk2_all = wgk2_all.at[:, gk_rank:2 * gk_rank, 128:256].set(w_gk2[:, 1]).astype(BF16)
    bgk_all = b_gk.reshape(depth, 1, 256)
    gla_g_all = gla_g.reshape(depth, 1, 256)
    grp = 256 // ng
    pw_bd_all = jnp.einsum("ab,lgcd->lgacbd", jnp.eye(POOL_PACK, dtype=F32), pool_w).reshape(
        depth, ng, 256, 256).astype(BF16)
    ps_all = jnp.tile(pool_scale.reshape(depth, ng, 1, grp), (1, 1, 1, POOL_PACK))
    sgu_lg_all = sgu_ln_g.reshape(depth, 1, 256)
    sgu_lb_all = sgu_ln_b.reshape(depth, 1, 256)
    sgu_w_all = sgu_w.astype(BF16)
    sgu_bm_all = jnp.repeat(jnp.swapaxes(sgu_b, 1, 2), grp, axis=2)
    wg_all = w_gate.astype(BF16)
    wb_all = w_branch.astype(BF16)
    wo_all = w_o.astype(BF16)
    ln_g4 = ln_g.reshape(depth, 2, 1, d)
    ln_b4 = ln_b.reshape(depth, 2, 1, d)
    ffn_all = (w_ffn_gate.astype(BF16), w_ffn_up.astype(BF16), w_ffn_down.astype(BF16))
    n_exp = w_router.shape[2]
    wr_all = jnp.zeros((w_router.shape[0], d, LANES), F32).at[:, :, :n_exp].set(w_router)
    exp_all = (w_exp_gate.astype(BF16), w_exp_up.astype(BF16), w_exp_down.astype(BF16))

    for l in range(depth):
        last = l == depth - 1
        mod = _Sel(mod4, l)
        win, wgk2, bgk = _Sel(win_all, l), _Sel(wgk2_all, l), _Sel(bgk_all, l)
        wg, wb, wo = _Sel(wg_all, l), _Sel(wb_all, l), _Sel(wo_all, l)
        lg0, lb0 = _Sel(ln_g4, l, 0), _Sel(ln_b4, l, 0)
        lg1, lb1 = _Sel(ln_g4, l, 1), _Sel(ln_b4, l, 1)

        hl, qk_l, v_l, r_l, p_l, f_l, s_l = _inproj(lat, xl, mod, win, wgk2, bgk)
        hc, qk_c, v_c, r_c, p_c, f_c, s_c = _inproj(cst, xc, mod, win, wgk2, bgk)
        ya_l, ya_c = _gla(bsz, t, tc, (qk_l, v_l, r_l), (qk_c, v_c, r_c), _Sel(gla_g_all, l))
        sg = (_Sel(sgu_lg_all, l), _Sel(sgu_lb_all, l), _Sel(sgu_w_all, l), _Sel(sgu_bm_all, l))
        yb_l = _pool(bsz, t, p_l, pool_tab_l, pw_bd_all, ps_all, l, GRID_W)
        yc_l = _fourier(bsz, t, f_l, chan_tabs, dft_l)
        yd_l = _sgu(lat, s_l, *sg)
        xl = _merge(lat, xl, hl, (ya_l, yb_l, yc_l, yd_l), mod, wg, wb, wo, lg0, lb0, alpha)
        if not last:
            yb_c = _pool(bsz, tc, p_c, pool_tab_c, pw_bd_all, ps_all, l, 1)
            yc_c = _fourier(bsz, tc, f_c, chan_tabs, dft_c)
            yd_c = _sgu(cst, s_c, *sg)
            xc = _merge(cst, xc, hc, (ya_c, yb_c, yc_c, yd_c), mod, wg, wb, wo, lg0, lb0, alpha)

        i = l // 2
        if l % 2 == 0:
            fw = tuple(_Sel(w, i) for w in ffn_all)
            xl = _ffn(lat, xl, mod, *fw, lg1, lb1, alpha)
            if not last:
                xc = _ffn(cst, xc, mod, *fw, lg1, lb1, alpha)
        else:
            ew = tuple(_Sel(w, i) for w in exp_all)
            xl = _moe(lat, xl, mod, _Sel(wr_all, i), *ew, lg1, lb1, alpha)
            if not last:
                xc = _moe(cst, xc, mod, _Sel(wr_all, i), *ew, lg1, lb1, alpha)
    return xl.reshape(bsz, t, d)
```

```python
import functools
import math

import jax
import jax.numpy as jnp
from jax import lax
from jax.experimental import pallas as pl
from jax.experimental.pallas import tpu as pltpu

F32 = jnp.float32
BF16 = jnp.bfloat16

GRID_W = 64
GLA_HEADS = 4
GK_TAU = 16.0
POOL_WINDOWS = (2, 4, 8, 16)
SGU_CHUNK = 128
TOP_K = 2
EPS = 1e-6

LANES = 128
SUBLANES = 8
VMEM_LIMIT_BYTES = 56 * 1024 * 1024

GLA_C = 128
GLA_UNROLL = 8


def _cparams(*sem):
    return pltpu.CompilerParams(dimension_semantics=sem, vmem_limit_bytes=VMEM_LIMIT_BYTES)


def _resident(shape):
    nd = len(shape)
    return pl.BlockSpec(shape, lambda *_: (0,) * nd, pipeline_mode=pl.Buffered(1))


class _Sel:
    def __init__(self, arr, *idx):
        self.arr, self.idx = arr, tuple(idx)

    @property
    def shape(self):
        return tuple(self.arr.shape[len(self.idx):])

    def spec(self):
        idx, tail = self.idx, self.shape
        return pl.BlockSpec((None,) * len(idx) + tail, lambda *_: idx + (0,) * len(tail),
                            pipeline_mode=pl.Buffered(1))


def _dot(a, b):
    return jnp.dot(a, b, preferred_element_type=F32)


def _dot_nt(a, b):
    return lax.dot_general(a, b, (((1,), (1,)), ((), ())), preferred_element_type=F32)


def _dot_tn(a, b):
    return lax.dot_general(a, b, (((0,), (0,)), ((), ())), preferred_element_type=F32)


def _split3(x):
    hi = x.astype(BF16)
    r1 = x - hi.astype(F32)
    mid = r1.astype(BF16)
    lo = (r1 - mid.astype(F32)).astype(BF16)
    return hi, mid, lo


def _dot_exact_rhs(x, w):
    hi, mid, lo = _split3(x)
    return _dot(hi, w) + _dot(mid, w) + _dot(lo, w)


def _dot_exact_lhs(w, x):
    hi, mid, lo = _split3(x)
    return _dot(w, hi) + _dot(w, mid) + _dot(w, lo)


def _ln(x):
    mu = jnp.mean(x, axis=-1, keepdims=True)
    xc = x - mu
    var = jnp.mean(xc * xc, axis=-1, keepdims=True)
    return xc * lax.rsqrt(var + EPS)


def _sigmoid(x):
    return 1.0 / (1.0 + jnp.exp(-x))


def _silu(x):
    return x * _sigmoid(x)


def _gelu_tanh(x):
    return 0.5 * x * (1.0 + jnp.tanh(math.sqrt(2.0 / math.pi) * (x + 0.044715 * (x * x * x))))


def _pick_tile(n, candidates=(1024, 512, 256, 128)):
    for c in candidates:
        if n % c == 0:
            return c
    raise ValueError(f"no tile for {n}")


def _mod_kernel(c_ref, w_ref, b_ref, o_ref):
    s = _silu(c_ref[...])
    sh, sm, sl = _split3(s)
    w = w_ref[0]
    wh = w.astype(BF16)
    wl = (w - wh.astype(F32)).astype(BF16)
    acc = _dot(sh, wh) + (_dot(sh, wl) + _dot(sm, wh)) + (_dot(sm, wl) + _dot(sl, wh))
    o_ref[0] = acc + b_ref[0]


def _modulation(cvec, w_mod, b_mod):
    depth, d, n6 = w_mod.shape
    rows = cvec.shape[0]
    tn = _pick_tile(n6, (1536, 1024, 512, 256, 128))
    return pl.pallas_call(
        _mod_kernel,
        grid=(depth, n6 // tn),
        in_specs=[
            pl.BlockSpec((rows, d), lambda l, j: (0, 0)),
            pl.BlockSpec((1, d, tn), lambda l, j: (l, 0, j)),
            pl.BlockSpec((1, 1, tn), lambda l, j: (l, 0, j)),
        ],
        out_specs=pl.BlockSpec((1, rows, tn), lambda l, j: (l, 0, j)),
        out_shape=jax.ShapeDtypeStruct((depth, rows, n6), F32),
        compiler_params=_cparams("parallel", "parallel"),
        name="modulation",
    )(cvec, w_mod, b_mod.reshape(depth, 1, n6))


class _Stream:
    def __init__(self, n_tok, seq, mod_row_of_tile, tm):
        self.n_tok, self.seq, self.tm = n_tok, seq, tm
        self.mod_row_of_tile = mod_row_of_tile
        self.n_tiles = n_tok // tm

    def mod_spec(self, mod):
        f = self.mod_row_of_tile
        (l,) = mod.idx
        return pl.BlockSpec((None, 1, 1, mod.shape[-1]), lambda i, *_: (l, f(i), 0, 0))

    def tok_spec(self, width):
        return pl.BlockSpec((self.tm, width), lambda i, *_: (i, 0))


def _lat_stream(bsz, t):
    tm = _pick_tile(t, (512, 256, 128))
    per = t // tm
    return _Stream(bsz * t, t, lambda i: i // per, tm)


def _ctx_stream(bsz, tc):
    tm = _pick_tile(bsz * tc, (512, 256, 128))
    return _Stream(bsz * tc, tc, lambda i: bsz, tm)


def _inproj_kernel(x_ref, mod_ref, win_ref, wgk2_ref, bgk_ref,
                   h_ref, qkla_ref, v_ref, r_ref, pool_ref, fft_ref, sgu_ref, *, d, q_scale):
    shift = mod_ref[0, :, 0:d]
    scale = mod_ref[0, :, d:2 * d]
    c = GLA_C
    tri = jnp.where(lax.broadcasted_iota(jnp.int32, (c, c), 1)
                    <= lax.broadcasted_iota(jnp.int32, (c, c), 0), 1.0, 0.0).astype(BF16)
    x = x_ref[...]
    hb = (_ln(x) * (1.0 + scale) + shift).astype(BF16)
    h_ref[...] = hb
    p = _dot(hb, win_ref[...])
    qkla_ref[:, 0:128] = p[:, 0:128] * q_scale
    qkla_ref[:, 128:256] = p[:, 128:256]
    v_ref[...] = p[:, 256:512].astype(BF16)
    r_ref[...] = p[:, 512:768].astype(BF16)
    pool_ref[...] = p[:, 768:1024].astype(BF16)
    fft_ref[...] = p[:, 1024:1280].astype(BF16)
    sgu_ref[...] = p[:, 1280:1792].astype(BF16)
    z = _dot(p[:, 1792:1920].astype(BF16), wgk2_ref[...]) + bgk_ref[...]
    la = (jnp.minimum(z, 0.0) - jnp.log(1.0 + jnp.exp(-jnp.abs(z)))) * (1.0 / GK_TAU)
    split = jnp.concatenate(_split3(la), axis=-1)
    for ch in range(x.shape[0] // c):
        rows = slice(ch * c, (ch + 1) * c)
        pp = _dot(tri, split[rows, :])
        pref = (pp[:, 0:256] + pp[:, 256:512]) + pp[:, 512:768]
        qkla_ref[rows, 256:384] = pref[:, 0:128]
        pb = pref[:, 128:256]
        qkla_ref[rows, 384:512] = pb[c - 1:c, :] - pb + la[rows, 128:256]


def _inproj(st, x, mod, win, wgk2, bgk):
    d = x.shape[1]
    outs = [
        jax.ShapeDtypeStruct((st.n_tok, d), BF16),
        jax.ShapeDtypeStruct((st.n_tok, 512), F32),
        jax.ShapeDtypeStruct((st.n_tok, 256), BF16),
        jax.ShapeDtypeStruct((st.n_tok, 256), BF16),
        jax.ShapeDtypeStruct((st.n_tok, 256), BF16),
        jax.ShapeDtypeStruct((st.n_tok, 256), BF16),
        jax.ShapeDtypeStruct((st.n_tok, 512), BF16),
    ]
    dk = 128 // GLA_HEADS
    return pl.pallas_call(
        functools.partial(_inproj_kernel, d=d, q_scale=dk ** -0.5),
        grid=(st.n_tiles,),
        in_specs=[st.tok_spec(d), st.mod_spec(mod), win.spec(), wgk2.spec(), bgk.spec()],
        out_specs=[st.tok_spec(o.shape[1]) for o in outs],
        out_shape=outs,
        compiler_params=_cparams("parallel"),
        name="in_proj",
    )(x, mod.arr, win.arr, wgk2.arr, bgk.arr)


def _gla_kernel(qk_l, v_l, r_l, qk_c, v_c, r_c, g_ref, y_l, y_c, ht, oacc_l, oacc_c,
                *, nch_l, nch_c):
    c = GLA_C
    nh = GLA_HEADS
    ii4 = lax.broadcasted_iota(jnp.int32, (nh * c, c), 0) & (c - 1)
    jj4 = lax.broadcasted_iota(jnp.int32, (nh * c, c), 1)
    lane_qk = lax.broadcasted_iota(jnp.int32, (c, 128), 1) // (128 // nh)
    lane_v = lax.broadcasted_iota(jnp.int32, (c, 256), 1) // (256 // nh)
    st_mask = (lax.broadcasted_iota(jnp.int32, (256, 128), 0) // (256 // nh)
               == lax.broadcasted_iota(jnp.int32, (256, 128), 1) // (128 // nh))

    def chunk(qk, v, oacc, idx, fwd):
        dirn = 0 if fwd else 1
        pair4 = (jj4 <= ii4) if fwd else (jj4 > ii4)
        mid, last = (c // 2 - 1, c - 1) if fwd else (c // 2, 0)
        la0 = 256 if fwd else 384
        rows = pl.ds(pl.multiple_of(idx * c, c), c)
        q = qk[rows, 0:128]
        k = qk[rows, 128:256]
        b = qk[rows, la0:la0 + 128]
        b_mid = b[mid:mid + 1, :]
        b_last = b[last:last + 1, :]
        qe = q * jnp.exp(b - b_mid)
        ke = (k * jnp.exp(b_mid - b)).astype(BF16)
        kt = (k * jnp.exp(b_last - b)).astype(BF16)
        qa = (qe * jnp.exp(b_mid)).astype(BF16)
        qs = jnp.concatenate([jnp.where(lane_qk == h, qe, 0.0) for h in range(nh)],
                             axis=0).astype(BF16)
        att = jnp.where(pair4, _dot_nt(qs, ke), 0.0).astype(BF16)
        vv = v[rows, :]
        os_ = _dot(att, vv)
        o = _dot_nt(qa, ht[dirn].astype(BF16))
        for h in range(nh):
            o = o + jnp.where(lane_v == h, os_[h * c:(h + 1) * c, :], 0.0)
        oacc[dirn, rows, :] = o
        dht = _dot_tn(vv, kt)
        ht[dirn] = ht[dirn] * jnp.exp(b_last) + jnp.where(st_mask, dht, 0.0)

    def run(qk, v, oacc, nch):
        unroll = math.gcd(nch, GLA_UNROLL)

        def step(n, carry):
            for u in range(unroll):
                chunk(qk, v, oacc, n * unroll + u, True)
                chunk(qk, v, oacc, nch - 1 - (n * unroll + u), False)
            return carry

        lax.fori_loop(0, nch // unroll, step, 0)

    ht[...] = jnp.zeros_like(ht)
    run(qk_c, v_c, oacc_c, nch_c)
    run(qk_l, v_l, oacc_l, nch_l)

    hv = 256 // nh
    mh = jnp.where(lax.broadcasted_iota(jnp.int32, (256, 256), 0) // hv
                   == lax.broadcasted_iota(jnp.int32, (256, 256), 1) // hv, 1.0 / hv, 0.0).astype(BF16)
    g = g_ref[...]

    def finish(oacc, r, y, nch):
        unroll = math.gcd(nch, GLA_UNROLL)

        def step(n, carry):
            for u in range(unroll):
                rows = pl.ds(pl.multiple_of((n * unroll + u) * c, c), c)
                o = oacc[0, rows, :] + oacc[1, rows, :]
                sq = o * o
                sh = sq.astype(BF16)
                sl = (sq - sh.astype(F32)).astype(BF16)
                ms = _dot(sh, mh) + _dot(sl, mh)
                rr = r[rows, :].astype(F32)
                y[rows, :] = (o * lax.rsqrt(ms + EPS) * g * _silu(rr)).astype(BF16)
            return carry

        lax.fori_loop(0, nch // unroll, step, 0)

    finish(oacc_c, r_c, y_c, nch_c)
    finish(oacc_l, r_l, y_l, nch_l)


def _gla(bsz, t, tc, lat, ctx, gla_g):
    c = GLA_C
    nch_l, nch_c = t // c, tc // c

    def prep(arrs, seq):
        qkla, v, r = arrs
        return (qkla.reshape(bsz, seq, 512), v.reshape(bsz, seq, 256), r.reshape(bsz, seq, 256))

    def specs(seq):
        return [pl.BlockSpec((None, seq, 512), lambda b: (b, 0, 0)),
                pl.BlockSpec((None, seq, 256), lambda b: (b, 0, 0)),
                pl.BlockSpec((None, seq, 256), lambda b: (b, 0, 0))]

    y_l, y_c = pl.pallas_call(
        functools.partial(_gla_kernel, nch_l=nch_l, nch_c=nch_c),
        grid=(bsz,),
        in_specs=specs(t) + specs(tc) + [gla_g.spec()],
        out_specs=[pl.BlockSpec((None, t, 256), lambda b: (b, 0, 0)),
                   pl.BlockSpec((None, tc, 256), lambda b: (b, 0, 0))],
        out_shape=[jax.ShapeDtypeStruct((bsz, t, 256), BF16),
                   jax.ShapeDtypeStruct((bsz, tc, 256), BF16)],
        scratch_shapes=[pltpu.VMEM((2, 256, 128), F32), pltpu.VMEM((2, t, 256), F32),
                        pltpu.VMEM((2, tc, 256), F32)],
        compiler_params=_cparams("parallel"),
        name="gla",
    )(*prep(lat, t), *prep(ctx, tc), gla_g.arr)
    return y_l.reshape(bsz * t, 256), y_c.reshape(bsz * tc, 256)


POOL_PACK = 4


def _pool_kernel(x_ref, s_ref, ic_ref, pw_ref, ps_ref, o_ref, xp_ref, op_ref, *, ng, seq):
    g = pl.program_id(1)
    grp = 256 // ng
    rc = min(seq, 512)

    @pl.when(g == 0)
    def _():
        def pack(c, carry):
            rows = pl.ds(pl.multiple_of(c * rc, rc), rc)
            for gg in range(ng):
                xp_ref[gg, rows, :] = jnp.concatenate(
                    [x_ref[b, rows, gg * grp:(gg + 1) * grp] for b in range(POOL_PACK)], axis=-1)
            return carry

        lax.fori_loop(0, seq // rc, pack, 0)

    x = xp_ref[g]
    ic = ic_ref[...]
    m = _dot(s_ref[...], x) * jnp.concatenate([ic, ic], axis=-1)
    dlt = (m - x.astype(F32)).astype(BF16)
    op_ref[g] = (_dot(dlt, pw_ref[...]) * ps_ref[...]).astype(BF16)

    @pl.when(g == ng - 1)
    def _():
        def unpack(c, carry):
            rows = pl.ds(pl.multiple_of(c * rc, rc), rc)
            for b in range(POOL_PACK):
                o_ref[b, rows, :] = jnp.concatenate(
                    [op_ref[gg, rows, b * grp:(b + 1) * grp] for gg in range(ng)], axis=-1)
            return carry

        lax.fori_loop(0, seq // rc, unpack, 0)


def _window_bounds(pos, n, win):
    return jnp.clip(pos - win // 2, 0, n), jnp.clip(pos + (win - win // 2), 0, n)


def _pool_tables(seq, rows):
    mats, invs = [], []
    tok = jnp.arange(seq)
    for win in POOL_WINDOWS:
        if rows is None:
            lo, hi = _window_bounds(tok, seq, win)
            member = (tok[None, :] >= lo[:, None]) & (tok[None, :] < hi[:, None])
            cnt = hi - lo
        else:
            rr, cc = tok // GRID_W, tok % GRID_W
            lo_r, hi_r = _window_bounds(rr, rows, win)
            lo_c, hi_c = _window_bounds(cc, GRID_W, win)
            member = ((rr[None, :] >= lo_r[:, None]) & (rr[None, :] < hi_r[:, None])
                      & (cc[None, :] >= lo_c[:, None]) & (cc[None, :] < hi_c[:, None]))
            cnt = (hi_r - lo_r) * (hi_c - lo_c)
        mats.append(member.astype(BF16))
        invs.append(jnp.broadcast_to((1.0 / cnt.astype(F32))[:, None], (seq, LANES)))
    return jnp.stack(mats), jnp.stack(invs)


def _pool(bsz, seq, xp, tables, pw_bd, ps_t, l):
    smat, inv = tables
    ng = len(POOL_WINDOWS)
    nq = bsz // POOL_PACK
    blk = pl.BlockSpec((POOL_PACK, seq, 256), lambda q, g: (q, 0, 0))
    out = pl.pallas_call(
        functools.partial(_pool_kernel, ng=ng, seq=seq),
        grid=(nq, ng),
        in_specs=[blk,
                  pl.BlockSpec((None, seq, seq), lambda q, g: (g, 0, 0)),
                  pl.BlockSpec((None, seq, LANES), lambda q, g: (g, 0, 0)),
                  pl.BlockSpec((None, None, 256, 256), lambda q, g: (l, g, 0, 0)),
                  pl.BlockSpec((None, None, 1, 256), lambda q, g: (l, g, 0, 0))],
        out_specs=blk,
        out_shape=jax.ShapeDtypeStruct((bsz, seq, 256), BF16),
        scratch_shapes=[pltpu.VMEM((ng, seq, 256), BF16), pltpu.VMEM((ng, seq, 256), BF16)],
        compiler_params=_cparams("arbitrary", "arbitrary"),
        name="pool",
    )(xp.reshape(bsz, seq, 256), smat, inv, pw_bd, ps_t)
    return out.reshape(bsz * seq, 256)


def _fourier_kernel(z_ref, cb_ref, sb_ref, ct_ref, st_ref, o_ref, *, scale):
    z = z_ref[...]
    zc = _dot(z, cb_ref[...]).astype(BF16)
    zs = _dot(z, sb_ref[...]).astype(BF16)
    y = _dot(ct_ref[...], zc) - _dot(st_ref[...], zs)
    o_ref[...] = (y * scale).astype(BF16)


def _dft_tables(n):
    k = jnp.arange(n, dtype=jnp.int32)
    ang = ((k[:, None] * k[None, :]) % n).astype(F32) * (2.0 * math.pi / n)
    return jnp.cos(ang), jnp.sin(ang)


def _fourier_chan_tables(ng):
    grp = 256 // ng
    cg, sg = _dft_tables(grp)
    eye = jnp.eye(ng, dtype=F32)
    return jnp.kron(eye, cg).astype(BF16), jnp.kron(eye, sg).astype(BF16)


def _fourier(bsz, seq, z, chan_tabs, seq_tabs):
    cb, sb = chan_tabs
    ct, st = seq_tabs
    grp = cb.shape[0] // 4
    out = pl.pallas_call(
        functools.partial(_fourier_kernel, scale=1.0 / math.sqrt(seq * grp)),
        grid=(bsz,),
        in_specs=[pl.BlockSpec((None, seq, 256), lambda b: (b, 0, 0)),
                  _resident((256, 256)), _resident((256, 256)),
                  _resident((seq, seq)), _resident((seq, seq))],
        out_specs=pl.BlockSpec((None, seq, 256), lambda b: (b, 0, 0)),
        out_shape=jax.ShapeDtypeStruct((bsz, seq, 256), BF16),
        compiler_params=_cparams("parallel"),
        name="fourier",
    )(z.reshape(bsz, seq, 256), cb, sb, ct, st)
    return out.reshape(bsz * seq, 256)


def _sgu_kernel(s_ref, lg_ref, lb_ref, w_ref, bm_ref, o_ref, *, n_chunks, ng):
    u = _gelu_tanh(s_ref[:, 0:256].astype(F32))
    v = _gelu_tanh(s_ref[:, 256:512].astype(F32))
    vb = (_ln(v) * lg_ref[...] + lb_ref[...]).astype(BF16)
    lane_g = lax.broadcasted_iota(jnp.int32, (SGU_CHUNK, 256), 1) // (256 // ng)
    for ch in range(n_chunks):
        rows = slice(ch * SGU_CHUNK, (ch + 1) * SGU_CHUNK)
        vc = vb[rows, :]
        s = bm_ref[...]
        for g in range(ng):
            s = s + jnp.where(lane_g == g, _dot(w_ref[g], vc), 0.0)
        o_ref[rows, :] = (u[rows, :] * s).astype(BF16)


def _sgu(st, sgu_in, lg, lb, w, bm):
    ng = w.shape[0]
    tm = st.tm
    return pl.pallas_call(
        functools.partial(_sgu_kernel, n_chunks=tm // SGU_CHUNK, ng=ng),
        grid=(st.n_tiles,),
        in_specs=[st.tok_spec(512), lg.spec(), lb.spec(), w.spec(), bm.spec()],
        out_specs=st.tok_spec(256),
        out_shape=jax.ShapeDtypeStruct((st.n_tok, 256), BF16),
        compiler_params=_cparams("parallel"),
        name="sgu",
    )(sgu_in, lg.arr, lb.arr, w.arr, bm.arr)


MERGE_COLS = 512


def _merge_kernel(x_ref, h_ref, y0, y1, y2, y3, mod_ref, wg_ref, wb_ref, wo_ref, lg_ref, lb_ref,
                  o_ref, *, d, alpha):
    h = h_ref[...]
    ys = (y0[...], y1[...], y2[...], y3[...])
    parts = []
    for cb in range(d // MERGE_COLS):
        cols = slice(cb * MERGE_COLS, (cb + 1) * MERGE_COLS)
        m = None
        for i in range(4):
            t = _sigmoid(_dot(h, wg_ref[i, :, cols])) * _dot(ys[i], wb_ref[i, :, cols])
            m = t if m is None else m + t
        parts.append(m.astype(BF16))
    y = _dot(jnp.concatenate(parts, axis=-1), wo_ref[...])
    gate = mod_ref[0, :, 2 * d:3 * d]
    xn = alpha * x_ref[...] + gate * y
    o_ref[...] = _ln(xn) * lg_ref[...] + lb_ref[...]


def _merge(st, x, h, ys, mod, wg, wb, wo, lg, lb, alpha):
    d = x.shape[1]
    return pl.pallas_call(
        functools.partial(_merge_kernel, d=d, alpha=alpha),
        grid=(st.n_tiles,),
        in_specs=[st.tok_spec(d), st.tok_spec(d)] + [st.tok_spec(256)] * 4
        + [st.mod_spec(mod), wg.spec(), wb.spec(), wo.spec(), lg.spec(), lb.spec()],
        out_specs=st.tok_spec(d),
        out_shape=jax.ShapeDtypeStruct((st.n_tok, d), F32),
        compiler_params=_cparams("parallel"),
        name="merge",
    )(x, h, *ys, mod.arr, wg.arr, wb.arr, wo.arr, lg.arr, lb.arr)


def _ff_blocks(ff):
    if ff <= 1536 or ff % 256:
        return [(0, ff)]
    first = (ff // 512 + (1 if (ff // 256) % 2 else 0)) * 256
    return [(0, first), (first, ff)]


def _ffn_kernel(x_ref, mod_ref, wg_ref, wu_ref, wd_ref, lg_ref, lb_ref, o_ref, *, d, alpha, blocks):
    x = x_ref[...]
    shift = mod_ref[0, :, 3 * d:4 * d]
    scale = mod_ref[0, :, 4 * d:5 * d]
    gate = mod_ref[0, :, 5 * d:6 * d]
    hb = (_ln(x) * (1.0 + scale) + shift).astype(BF16)
    y = None
    for lo, hi in blocks:
        a = (_silu(_dot(hb, wg_ref[:, lo:hi])) * _dot(hb, wu_ref[:, lo:hi])).astype(BF16)
        t = _dot(a, wd_ref[lo:hi, :])
        y = t if y is None else y + t
    xn = alpha * x + gate * y
    o_ref[...] = _ln(xn) * lg_ref[...] + lb_ref[...]


def _ffn(st, x, mod, wg, wu, wd, lg, lb, alpha):
    d = x.shape[1]
    ff = wg.shape[1]
    return pl.pallas_call(
        functools.partial(_ffn_kernel, d=d, alpha=alpha, blocks=_ff_blocks(ff)),
        grid=(st.n_tiles,),
        in_specs=[st.tok_spec(d), st.mod_spec(mod), wg.spec(), wu.spec(), wd.spec(),
                  lg.spec(), lb.spec()],
        out_specs=st.tok_spec(d),
        out_shape=jax.ShapeDtypeStruct((st.n_tok, d), F32),
        compiler_params=_cparams("parallel"),
        name="ffn",
    )(x, mod.arr, wg.arr, wu.arr, wd.arr, lg.arr, lb.arr)


MOE_PIECE = 16
MOE_ROW_TILE = 512
MOE_META = LANES


def _moe_local_rows(tm, n_exp):
    return -(-(TOP_K * tm + n_exp * (MOE_PIECE - 1)) // LANES) * LANES


ROUTE_ROWS = SUBLANES


def _router_kernel(x_ref, mod_ref, wr_ref, hb_ref, rt_ref, cnt_ref, *, d, n_exp, tm):
    i = pl.program_id(0)
    x = x_ref[...]
    shift = mod_ref[0, :, 3 * d:4 * d]
    scale = mod_ref[0, :, 4 * d:5 * d]
    h = _ln(x) * (1.0 + scale) + shift
    hb = h.astype(BF16)
    hb_ref[...] = hb

    wr = wr_ref[...]
    wrh = wr.astype(BF16)
    wrl = (wr - wrh.astype(F32)).astype(BF16)
    hl = (h - hb.astype(F32)).astype(BF16)
    logits = _dot(hb, wrh) + (_dot(hb, wrl) + _dot(hl, wrh))
    lt = logits.T[0:n_exp, :]
    sub = lax.broadcasted_iota(jnp.int32, (n_exp, tm), 0).astype(F32)
    neg = jnp.float32(-jnp.inf)
    v1 = jnp.max(lt, axis=0, keepdims=True)
    i1 = jnp.min(jnp.where(lt == v1, sub, float(n_exp)), axis=0, keepdims=True)
    rest = jnp.where(sub == i1, neg, lt)
    v2 = jnp.max(rest, axis=0, keepdims=True)
    i2 = jnp.min(jnp.where(rest == v2, sub, float(n_exp)), axis=0, keepdims=True)
    e2 = jnp.exp(v2 - v1)
    w1 = 1.0 / (1.0 + e2)
    w2 = e2 / (1.0 + e2)
    oht = jnp.where(sub == i1, 1.0, jnp.where(sub == i2, 1.0, 0.0))
    cnt_col = jnp.sum(oht, axis=1, keepdims=True)
    for e in range(n_exp):
        cnt_ref[i, e] = cnt_col[e, 0].astype(jnp.int32)
    rec = lax.broadcasted_iota(jnp.int32, (ROUTE_ROWS, tm), 0)
    rt_ref[...] = jnp.where(rec == 0, i1, jnp.where(rec == 1, i2, jnp.where(
        rec == 2, w1, jnp.where(rec == 3, w2, 0.0))))


def _dispatch_kernel(seg_ref, zf_ref, hb_ref, rt_ref, u_ref, xs_hbm, meta_ref,
                     sorted_ref, zero_ref, ztile_ref, sem, *, d, n_exp, tm, ls, n_tiles, tf):
    i = pl.program_id(0)
    hb = hb_ref[...]
    i1 = rt_ref[0:1, :]
    i2 = rt_ref[1:2, :]
    w1 = rt_ref[2:3, :]
    w2 = rt_ref[3:4, :]
    sub = lax.broadcasted_iota(jnp.int32, (n_exp, tm), 0).astype(F32)
    oh1 = sub == i1
    oh2 = sub == i2
    oht = jnp.where(oh1, 1.0, jnp.where(oh2, 1.0, 0.0))
    cumt = _dot(oht.astype(BF16), u_ref[...])
    cnts = [seg_ref[i, n_exp + e] for e in range(n_exp)]
    npc = [(cn + (MOE_PIECE - 1)) // MOE_PIECE for cn in cnts]
    los = []
    lo = jnp.int32(0)
    for e in range(n_exp):
        los.append(lo)
        lo = lo + npc[e] * MOE_PIECE
    lo_b = jnp.zeros((n_exp, tm), F32)
    for e in range(n_exp):
        lo_b = jnp.where(sub == float(e), los[e].astype(F32), lo_b)
    lpt = cumt + lo_b
    lp1 = jnp.sum(jnp.where(oh1, lpt, 0.0), axis=0, keepdims=True)
    lp2 = jnp.sum(jnp.where(oh2, lpt, 0.0), axis=0, keepdims=True)

    row = lax.broadcasted_iota(jnp.int32, (ls, tm), 0).astype(F32)
    m1 = row == lp1
    m2 = row == lp2
    perm = jnp.where(m1, 1.0, jnp.where(m2, 1.0, 0.0)).astype(BF16)
    gcol = jnp.sum(jnp.where(m1, w1, jnp.where(m2, w2, 0.0)), axis=1, keepdims=True)
    ghi = gcol.astype(BF16).astype(F32)
    r1 = gcol - ghi
    gmid = r1.astype(BF16).astype(F32)
    glo = r1 - gmid
    lane = lax.broadcasted_iota(jnp.int32, (ls, MOE_META), 1)
    gate_lanes = jnp.where(
        lane == 0, ghi, jnp.where(lane == 1, gmid, jnp.where(lane == 2, glo, 0.0))).astype(BF16)

    def piece(src_ref, src_row, dst_row, s):
        return pltpu.make_async_copy(
            src_ref.at[pl.ds(pl.multiple_of(src_row, MOE_PIECE), MOE_PIECE), :],
            xs_hbm.at[pl.ds(pl.multiple_of(dst_row, MOE_PIECE), MOE_PIECE), :], sem.at[s])

    def drain(tile):
        n = jnp.int32(0)
        for e in range(n_exp):
            n = n + (seg_ref[tile, n_exp + e] + (MOE_PIECE - 1)) // MOE_PIECE

        def wait(j, carry):
            piece(sorted_ref, 0, 0, 0).wait()
            return carry

        lax.fori_loop(0, n, wait, 0)

    @pl.when(i >= 1)
    def _():
        drain(i - 1)

    sorted_ref[:, 0:d] = _dot(perm, hb).astype(BF16)
    sorted_ref[:, d:d + MOE_META] = gate_lanes

    rows8 = jnp.where(sub == 0.0, lp1, jnp.where(sub == 1.0, lp2, 0.0))
    padded = jnp.concatenate([rows8, jnp.zeros((LANES - n_exp, tm), F32)], axis=0)
    meta_ref[...] = padded.T

    for e in range(n_exp):
        base = seg_ref[i, e]

        def start(j, carry, e=e, base=base):
            piece(sorted_ref, los[e] + j * MOE_PIECE, base + j * MOE_PIECE, 0).start()
            return carry

        lax.fori_loop(0, npc[e], start, 0)

    @pl.when(i == n_tiles - 1)
    def _():
        drain(i)
        zero_ref[...] = jnp.zeros_like(zero_ref)
        ztile_ref[...] = jnp.zeros_like(ztile_ref)
        n_zero = jnp.int32(0)
        for e in range(n_exp):
            z0 = zf_ref[2 * e]
            nz = zf_ref[2 * e + 1]

            def zstart(j, carry, z0=z0):
                piece(zero_ref, 0, z0 + j * MOE_PIECE, 1).start()
                return carry

            lax.fori_loop(0, nz, zstart, 0)
            n_zero = n_zero + nz

        def zwait(j, carry):
            piece(zero_ref, 0, 0, 1).wait()
            return carry

        lax.fori_loop(0, n_zero, zwait, 0)

        def tile_copy(t):
            return pltpu.make_async_copy(
                ztile_ref, xs_hbm.at[pl.ds(pl.multiple_of(t * tf, tf), tf), :], sem.at[1])

        t0 = zf_ref[2 * n_exp]
        t1 = zf_ref[2 * n_exp + 1]

        def tstart(t, carry):
            tile_copy(t).start()
            return carry

        def twait(t, carry):
            tile_copy(t).wait()
            return carry

        lax.fori_loop(t0, t1, tstart, 0)
        lax.fori_loop(t0, t1, twait, 0)


def _expert_kernel(te_ref, na_ref, xs_ref, wg_ref, wu_ref, wd_ref, ys_ref, *, d, blocks):
    active = pl.program_id(0) < na_ref[0]

    @pl.when(jnp.logical_not(active))
    def _():
        ys_ref[...] = jnp.zeros_like(ys_ref)

    @pl.when(active)
    def _():
        hb = xs_ref[:, 0:d]
        gate = jnp.sum(xs_ref[:, d:d + MOE_META].astype(F32), axis=-1, keepdims=True)
        y = None
        for lo, hi in blocks:
            a = (_silu(_dot(hb, wg_ref[:, lo:hi])) * _dot(hb, wu_ref[:, lo:hi])).astype(BF16)
            t = _dot(a, wd_ref[lo:hi, :])
            y = t if y is None else y + t
        ys_ref[...] = (y * gate).astype(BF16)


def _combine_kernel(seg_ref, x_ref, mod_ref, meta_ref, ys_hbm, lg_ref, lb_ref, o_ref, ysl_ref, sem,
                    *, d, alpha, n_exp, tm, ls, n_tiles):
    i = pl.program_id(0)

    def piece(src_row, dst_row, s):
        return pltpu.make_async_copy(
            ys_hbm.at[pl.ds(pl.multiple_of(src_row, MOE_PIECE), MOE_PIECE), :],
            ysl_ref.at[s, pl.ds(pl.multiple_of(dst_row, MOE_PIECE), MOE_PIECE), :], sem.at[s])

    def n_pieces(tile):
        n = jnp.int32(0)
        for e in range(n_exp):
            n = n + (seg_ref[tile, n_exp + e] + (MOE_PIECE - 1)) // MOE_PIECE
        return n

    def fetch(tile):
        s = tile % 2
        lo = jnp.int32(0)
        for e in range(n_exp):
            base = seg_ref[tile, e]
            npc = (seg_ref[tile, n_exp + e] + (MOE_PIECE - 1)) // MOE_PIECE

            def start(j, carry, base=base, lo=lo):
                piece(base + j * MOE_PIECE, lo + j * MOE_PIECE, s).start()
                return carry

            lax.fori_loop(0, npc, start, 0)
            lo = lo + npc * MOE_PIECE

        def clear(j, carry):
            ysl_ref[s, pl.ds(pl.multiple_of(j * MOE_PIECE, MOE_PIECE), MOE_PIECE), :] = jnp.zeros(
                (MOE_PIECE, d), BF16)
            return carry

        lax.fori_loop(lo // MOE_PIECE, ls // MOE_PIECE, clear, 0)

    @pl.when(i == 0)
    def _():
        fetch(i)

    @pl.when(i + 1 < n_tiles)
    def _():
        fetch(i + 1)

    slot = i % 2

    def wait(j, carry):
        piece(0, 0, slot).wait()
        return carry

    lax.fori_loop(0, n_pieces(i), wait, 0)

    lp1 = meta_ref[:, 0:1]
    lp2 = meta_ref[:, 1:2]
    col = lax.broadcasted_iota(jnp.int32, (tm, ls), 1).astype(F32)
    perm_t = jnp.where(col == lp1, 1.0, jnp.where(col == lp2, 1.0, 0.0)).astype(BF16)
    y = _dot(perm_t, ysl_ref[slot])
    gate = mod_ref[0, :, 5 * d:6 * d]
    xn = alpha * x_ref[...] + gate * y
    o_ref[...] = _ln(xn) * lg_ref[...] + lb_ref[...]


def _moe(st, x, mod, wr_pad, wg, wu, wd, lg, lb, alpha):
    d = x.shape[1]
    n_exp, _, ff = wg.shape
    (li,) = wg.idx
    tm, n_tiles, n_tok = st.tm, st.n_tiles, st.n_tok
    ls = _moe_local_rows(tm, n_exp)
    tf = MOE_ROW_TILE
    dx = d + MOE_META
    i32 = jnp.int32

    hb, rt, cnt = pl.pallas_call(
        functools.partial(_router_kernel, d=d, n_exp=n_exp, tm=tm),
        grid=(n_tiles,),
        in_specs=[st.tok_spec(d), st.mod_spec(mod), wr_pad.spec()],
        out_specs=[st.tok_spec(d),
                   pl.BlockSpec((None, ROUTE_ROWS, tm), lambda i: (i, 0, 0)),
                   pl.BlockSpec(memory_space=pltpu.SMEM)],
        out_shape=[jax.ShapeDtypeStruct((n_tok, d), BF16),
                   jax.ShapeDtypeStruct((n_tiles, ROUTE_ROWS, tm), F32),
                   jax.ShapeDtypeStruct((n_tiles, n_exp), i32)],
        compiler_params=_cparams("arbitrary"),
        name="moe_router",
    )(x, mod.arr, wr_pad.arr)

    padded = -(-cnt // MOE_PIECE) * MOE_PIECE
    per_e = jnp.sum(padded, axis=0)
    per_e_full = -(-per_e // tf) * tf
    region_end = jnp.cumsum(per_e_full)
    region_start = region_end - per_e_full
    base = region_start[None, :] + jnp.cumsum(padded, axis=0) - padded
    seg = jnp.concatenate([base, cnt], axis=1).astype(i32)
    n_row_tiles = -(-(TOP_K * n_tok + n_tiles * n_exp * (MOE_PIECE - 1) + n_exp * (tf - 1)) // tf)
    tile_end = region_end // tf
    n_act = tile_end[-1]
    zfill = jnp.concatenate([
        jnp.stack([region_start + per_e, (per_e_full - per_e) // MOE_PIECE], axis=1).reshape(-1),
        jnp.stack([n_act, jnp.asarray(n_row_tiles, i32)])]).astype(i32)
    t_idx = jnp.minimum(jnp.arange(n_row_tiles, dtype=i32), n_act - 1)
    t_e = jnp.sum(t_idx[:, None] >= tile_end[None, :], axis=1).astype(i32)
    upper = (jnp.arange(tm)[:, None] < jnp.arange(tm)[None, :]).astype(BF16)

    xs, meta = pl.pallas_call(
        functools.partial(_dispatch_kernel, d=d, n_exp=n_exp, tm=tm, ls=ls, n_tiles=n_tiles, tf=tf),
        grid_spec=pltpu.PrefetchScalarGridSpec(
            num_scalar_prefetch=2,
            grid=(n_tiles,),
            in_specs=[st.tok_spec(d),
                      pl.BlockSpec((None, ROUTE_ROWS, tm), lambda i, *_: (i, 0, 0)),
                      _resident((tm, tm))],
            out_specs=[pl.BlockSpec(memory_space=pl.ANY), st.tok_spec(LANES)],
            scratch_shapes=[pltpu.VMEM((ls, dx), BF16), pltpu.VMEM((MOE_PIECE, dx), BF16),
                            pltpu.VMEM((tf, dx), BF16), pltpu.SemaphoreType.DMA((2,))],
        ),
        out_shape=[jax.ShapeDtypeStruct((n_row_tiles * tf, dx), BF16),
                   jax.ShapeDtypeStruct((n_tok, LANES), F32)],
        compiler_params=_cparams("arbitrary"),
        name="moe_dispatch",
    )(seg, zfill, hb, rt, upper)

    ys = pl.pallas_call(
        functools.partial(_expert_kernel, d=d, blocks=_ff_blocks(ff)),
        grid_spec=pltpu.PrefetchScalarGridSpec(
            num_scalar_prefetch=2,
            grid=(n_row_tiles,),
            in_specs=[pl.BlockSpec((tf, dx), lambda i, te, na: (i, 0)),
                      pl.BlockSpec((None, None, d, ff), lambda i, te, na: (li, te[i], 0, 0)),
                      pl.BlockSpec((None, None, d, ff), lambda i, te, na: (li, te[i], 0, 0)),
                      pl.BlockSpec((None, None, ff, d), lambda i, te, na: (li, te[i], 0, 0))],
            out_specs=pl.BlockSpec((tf, d), lambda i, te, na: (i, 0)),
        ),
        out_shape=jax.ShapeDtypeStruct((n_row_tiles * tf, d), BF16),
        compiler_params=_cparams("arbitrary"),
        name="moe_experts",
    )(t_e, n_act.reshape(1).astype(i32), xs, wg.arr, wu.arr, wd.arr)

    return pl.pallas_call(
        functools.partial(_combine_kernel, d=d, alpha=alpha, n_exp=n_exp, tm=tm, ls=ls,
                          n_tiles=n_tiles),
        grid_spec=pltpu.PrefetchScalarGridSpec(
            num_scalar_prefetch=1,
            grid=(n_tiles,),
            in_specs=[st.tok_spec(d), st.mod_spec(mod), st.tok_spec(LANES),
                      pl.BlockSpec(memory_space=pl.ANY), lg.spec(), lb.spec()],
            out_specs=st.tok_spec(d),
            scratch_shapes=[pltpu.VMEM((2, ls, d), BF16), pltpu.SemaphoreType.DMA((2,))],
        ),
        out_shape=jax.ShapeDtypeStruct((n_tok, d), F32),
        compiler_params=_cparams("arbitrary"),
        name="moe_combine",
    )(seg, x, mod.arr, meta, ys, lg.arr, lb.arr)


def kernel(x, c, ctx, c_ctx, w_mod, b_mod, w_in, w_gk2, b_gk, gla_g, pool_w, pool_scale, sgu_ln_g, sgu_ln_b, sgu_w, sgu_b, w_gate, w_branch, w_o, ln_g, ln_b, w_ffn_gate, w_ffn_up, w_ffn_down, w_router, w_exp_gate, w_exp_up, w_exp_down):
    bsz, t, d = x.shape
    tc = ctx.shape[1]
    depth = w_mod.shape[0]
    n6 = w_mod.shape[2]
    alpha = (2 * depth) ** 0.25
    rows = t // GRID_W
    ng = len(POOL_WINDOWS)
    gk_rank = w_gk2.shape[2]
    assert d == 1024 and w_in.shape[2] == 1792 + 2 * gk_rank and 2 * gk_rank <= LANES
    assert t % GLA_C == 0 and tc % GLA_C == 0 and bsz % POOL_PACK == 0

    n_rows = -(-(bsz + 1) // SUBLANES) * SUBLANES
    cvec = jnp.zeros((n_rows, d), F32).at[:bsz].set(c).at[bsz].set(c_ctx)
    mod_all = _modulation(cvec, w_mod, b_mod)

    lat = _lat_stream(bsz, t)
    cst = _ctx_stream(bsz, tc)
    xl = x.reshape(bsz * t, d)
    xc = ctx.reshape(bsz * tc, d)

    pool_tab_l = _pool_tables(t, rows)
    pool_tab_c = _pool_tables(tc, None)
    chan_tabs = _fourier_chan_tables(ng)
    dft_l = tuple(a.astype(BF16) for a in _dft_tables(t))
    dft_c = tuple(a.astype(BF16) for a in _dft_tables(tc))

    gk0 = 768
    gk1 = gk0 + 2 * gk_rank

    mod4 = mod_all.reshape(depth, n_rows, 1, n6)
    win_all = jnp.concatenate(
        [w_in[:, :, :gk0], w_in[:, :, gk1:], w_in[:, :, gk0:gk1],
         jnp.zeros((depth, d, LANES - 2 * gk_rank), F32)], axis=2).astype(BF16)
    wgk2_all = jnp.zeros((depth, LANES, 256), F32)
    wgk2_all = wgk2_all.at[:, 0:gk_rank, 0:128].set(w_gk2[:, 0])
    wgk2_all = wgk2_all.at[:, gk_rank:2 * gk_rank, 128:256].set(w_gk2[:, 1]).astype(BF16)
    bgk_all = b_gk.reshape(depth, 1, 256)
    gla_g_all = gla_g.reshape(depth, 1, 256)
    grp = 256 // ng
    pw_bd_all = jnp.einsum("ab,lgcd->lgacbd", jnp.eye(POOL_PACK, dtype=F32), pool_w).reshape(
        depth, ng, 256, 256).astype(BF16)
    ps_all = jnp.tile(pool_scale.reshape(depth, ng, 1, grp), (1, 1, 1, POOL_PACK))
    sgu_lg_all = sgu_ln_g.reshape(depth, 1, 256)
    sgu_lb_all = sgu_ln_b.reshape(depth, 1, 256)
    sgu_w_all = sgu_w.astype(BF16)
    sgu_bm_all = jnp.repeat(jnp.swapaxes(sgu_b, 1, 2), grp, axis=2)
    wg_all = w_gate.astype(BF16)
    wb_all = w_branch.astype(BF16)
    wo_all = w_o.astype(BF16)
    ln_g4 = ln_g.reshape(depth, 2, 1, d)
    ln_b4 = ln_b.reshape(depth, 2, 1, d)
    ffn_all = (w_ffn_gate.astype(BF16), w_ffn_up.astype(BF16), w_ffn_down.astype(BF16))
    n_exp = w_router.shape[2]
    wr_all = jnp.zeros((w_router.shape[0], d, LANES), F32).at[:, :, :n_exp].set(w_router)
    exp_all = (w_exp_gate.astype(BF16), w_exp_up.astype(BF16), w_exp_down.astype(BF16))

    for l in range(depth):
        last = l == depth - 1
        mod = _Sel(mod4, l)
        win, wgk2, bgk = _Sel(win_all, l), _Sel(wgk2_all, l), _Sel(bgk_all, l)
        wg, wb, wo = _Sel(wg_all, l), _Sel(wb_all, l), _Sel(wo_all, l)
        lg0, lb0 = _Sel(ln_g4, l, 0), _Sel(ln_b4, l, 0)
        lg1, lb1 = _Sel(ln_g4, l, 1), _Sel(ln_b4, l, 1)

        hl, qk_l, v_l, r_l, p_l, f_l, s_l = _inproj(lat, xl, mod, win, wgk2, bgk)
        hc, qk_c, v_c, r_c, p_c, f_c, s_c = _inproj(cst, xc, mod, win, wgk2, bgk)
        ya_l, ya_c = _gla(bsz, t, tc, (qk_l, v_l, r_l), (qk_c, v_c, r_c), _Sel(gla_g_all, l))
        sg = (_Sel(sgu_lg_all, l), _Sel(sgu_lb_all, l), _Sel(sgu_w_all, l), _Sel(sgu_bm_all, l))
        yb_l = _pool(bsz, t, p_l, pool_tab_l, pw_bd_all, ps_all, l)
        yc_l = _fourier(bsz, t, f_l, chan_tabs, dft_l)
        yd_l = _sgu(lat, s_l, *sg)
        xl = _merge(lat, xl, hl, (ya_l, yb_l, yc_l, yd_l), mod, wg, wb, wo, lg0, lb0, alpha)
        if not last:
            yb_c = _pool(bsz, tc, p_c, pool_tab_c, pw_bd_all, ps_all, l)
            yc_c = _fourier(bsz, tc, f_c, chan_tabs, dft_c)
            yd_c = _sgu(cst, s_c, *sg)
            xc = _merge(cst, xc, hc, (ya_c, yb_c, yc_c, yd_c), mod, wg, wb, wo, lg0, lb0, alpha)

        i = l // 2
        if l % 2 == 0:
            fw = tuple(_Sel(w, i) for w in ffn_all)
            xl = _ffn(lat, xl, mod, *fw, lg1, lb1, alpha)
            if not last:
                xc = _ffn(cst, xc, mod, *fw, lg1, lb1, alpha)
        else:
            ew = tuple(_Sel(w, i) for w in exp_all)
            xl = _moe(lat, xl, mod, _Sel(wr_all, i), *ew, lg1, lb1, alpha)
            if not last:
                xc = _moe(cst, xc, mod, _Sel(wr_all, i), *ew, lg1, lb1, alpha)
    return xl.reshape(bsz, t, d)
```

```python
import functools
import math

import jax
import jax.numpy as jnp
from jax import lax
from jax.experimental import pallas as pl
from jax.experimental.pallas import tpu as pltpu

F32 = jnp.float32
BF16 = jnp.bfloat16

GRID_W = 64
GLA_HEADS = 4
GK_TAU = 16.0
POOL_WINDOWS = (2, 4, 8, 16)
SGU_CHUNK = 128
TOP_K = 2
EPS = 1e-6

LANES = 128
SUBLANES = 8
VMEM_LIMIT_BYTES = 56 * 1024 * 1024

GLA_C = 128
GLA_UNROLL = 16


def _cparams(*sem):
    return pltpu.CompilerParams(dimension_semantics=sem, vmem_limit_bytes=VMEM_LIMIT_BYTES)


def _resident(shape):
    nd = len(shape)
    return pl.BlockSpec(shape, lambda *_: (0,) * nd, pipeline_mode=pl.Buffered(1))


class _Sel:
    def __init__(self, arr, *idx):
        self.arr, self.idx = arr, tuple(idx)

    @property
    def shape(self):
        return tuple(self.arr.shape[len(self.idx):])

    def spec(self):
        idx, tail = self.idx, self.shape
        return pl.BlockSpec((None,) * len(idx) + tail, lambda *_: idx + (0,) * len(tail),
                            pipeline_mode=pl.Buffered(1))


def _dot(a, b):
    return jnp.dot(a, b, preferred_element_type=F32)


def _dot_nt(a, b):
    return lax.dot_general(a, b, (((1,), (1,)), ((), ())), preferred_element_type=F32)


def _dot_tn(a, b):
    return lax.dot_general(a, b, (((0,), (0,)), ((), ())), preferred_element_type=F32)


def _split3(x):
    hi = x.astype(BF16)
    r1 = x - hi.astype(F32)
    mid = r1.astype(BF16)
    lo = (r1 - mid.astype(F32)).astype(BF16)
    return hi, mid, lo


def _dot_exact_rhs(x, w):
    hi, mid, lo = _split3(x)
    return _dot(hi, w) + _dot(mid, w) + _dot(lo, w)


def _dot_exact_lhs(w, x):
    hi, mid, lo = _split3(x)
    return _dot(w, hi) + _dot(w, mid) + _dot(w, lo)


def _ln(x):
    mu = jnp.mean(x, axis=-1, keepdims=True)
    xc = x - mu
    var = jnp.mean(xc * xc, axis=-1, keepdims=True)
    return xc * lax.rsqrt(var + EPS)


def _sigmoid(x):
    return 1.0 / (1.0 + jnp.exp(-x))


def _silu(x):
    return x * _sigmoid(x)


def _gelu_tanh(x):
    return 0.5 * x * (1.0 + jnp.tanh(math.sqrt(2.0 / math.pi) * (x + 0.044715 * (x * x * x))))


def _pick_tile(n, candidates=(1024, 512, 256, 128)):
    for c in candidates:
        if n % c == 0:
            return c
    raise ValueError(f"no tile for {n}")


def _mod_kernel(c_ref, w_ref, b_ref, o_ref):
    s = _silu(c_ref[...])
    sh, sm, sl = _split3(s)
    w = w_ref[0]
    wh = w.astype(BF16)
    wl = (w - wh.astype(F32)).astype(BF16)
    acc = _dot(sh, wh) + (_dot(sh, wl) + _dot(sm, wh)) + (_dot(sm, wl) + _dot(sl, wh))
    o_ref[0] = acc + b_ref[0]


def _modulation(cvec, w_mod, b_mod):
    depth, d, n6 = w_mod.shape
    rows = cvec.shape[0]
    tn = _pick_tile(n6, (1536, 1024, 512, 256, 128))
    return pl.pallas_call(
        _mod_kernel,
        grid=(depth, n6 // tn),
        in_specs=[
            pl.BlockSpec((rows, d), lambda l, j: (0, 0)),
            pl.BlockSpec((1, d, tn), lambda l, j: (l, 0, j)),
            pl.BlockSpec((1, 1, tn), lambda l, j: (l, 0, j)),
        ],
        out_specs=pl.BlockSpec((1, rows, tn), lambda l, j: (l, 0, j)),
        out_shape=jax.ShapeDtypeStruct((depth, rows, n6), F32),
        compiler_params=_cparams("parallel", "parallel"),
        name="modulation",
    )(cvec, w_mod, b_mod.reshape(depth, 1, n6))


class _Stream:
    def __init__(self, n_tok, seq, mod_row_of_tile, tm):
        self.n_tok, self.seq, self.tm = n_tok, seq, tm
        self.mod_row_of_tile = mod_row_of_tile
        self.n_tiles = n_tok // tm

    def mod_spec(self, mod):
        f = self.mod_row_of_tile
        (l,) = mod.idx
        return pl.BlockSpec((None, 1, 1, mod.shape[-1]), lambda i, *_: (l, f(i), 0, 0))

    def tok_spec(self, width):
        return pl.BlockSpec((self.tm, width), lambda i, *_: (i, 0))


def _lat_stream(bsz, t):
    tm = _pick_tile(t, (512, 256, 128))
    per = t // tm
    return _Stream(bsz * t, t, lambda i: i // per, tm)


def _ctx_stream(bsz, tc):
    tm = _pick_tile(bsz * tc, (512, 256, 128))
    return _Stream(bsz * tc, tc, lambda i: bsz, tm)


def _inproj_kernel(x_ref, mod_ref, win_ref, wgk2_ref, bgk_ref,
                   h_ref, qkla_ref, v_ref, r_ref, pool_ref, fft_ref, sgu_ref, *, d, q_scale):
    shift = mod_ref[0, :, 0:d]
    scale = mod_ref[0, :, d:2 * d]
    c = GLA_C
    tri = jnp.where(lax.broadcasted_iota(jnp.int32, (c, c), 1)
                    <= lax.broadcasted_iota(jnp.int32, (c, c), 0), 1.0, 0.0).astype(BF16)
    x = x_ref[...]
    hb = (_ln(x) * (1.0 + scale) + shift).astype(BF16)
    h_ref[...] = hb
    p = _dot(hb, win_ref[...])
    qkla_ref[:, 0:128] = p[:, 0:128] * q_scale
    qkla_ref[:, 128:256] = p[:, 128:256]
    v_ref[...] = p[:, 256:512].astype(BF16)
    r_ref[...] = p[:, 512:768].astype(BF16)
    pool_ref[...] = p[:, 768:1024].astype(BF16)
    fft_ref[...] = p[:, 1024:1280].astype(BF16)
    sgu_ref[...] = p[:, 1280:1792].astype(BF16)
    z = _dot(p[:, 1792:1920].astype(BF16), wgk2_ref[...]) + bgk_ref[...]
    la = (jnp.minimum(z, 0.0) - jnp.log(1.0 + jnp.exp(-jnp.abs(z)))) * (1.0 / GK_TAU)
    split = jnp.concatenate(_split3(la), axis=-1)
    for ch in range(x.shape[0] // c):
        rows = slice(ch * c, (ch + 1) * c)
        pp = _dot(tri, split[rows, :])
        pref = (pp[:, 0:256] + pp[:, 256:512]) + pp[:, 512:768]
        qkla_ref[rows, 256:384] = pref[:, 0:128]
        pb = pref[:, 128:256]
        qkla_ref[rows, 384:512] = pb[c - 1:c, :] - pb + la[rows, 128:256]


def _inproj(st, x, mod, win, wgk2, bgk):
    d = x.shape[1]
    outs = [
        jax.ShapeDtypeStruct((st.n_tok, d), BF16),
        jax.ShapeDtypeStruct((st.n_tok, 512), F32),
        jax.ShapeDtypeStruct((st.n_tok, 256), BF16),
        jax.ShapeDtypeStruct((st.n_tok, 256), BF16),
        jax.ShapeDtypeStruct((st.n_tok, 256), BF16),
        jax.ShapeDtypeStruct((st.n_tok, 256), BF16),
        jax.ShapeDtypeStruct((st.n_tok, 512), BF16),
    ]
    dk = 128 // GLA_HEADS
    return pl.pallas_call(
        functools.partial(_inproj_kernel, d=d, q_scale=dk ** -0.5),
        grid=(st.n_tiles,),
        in_specs=[st.tok_spec(d), st.mod_spec(mod), win.spec(), wgk2.spec(), bgk.spec()],
        out_specs=[st.tok_spec(o.shape[1]) for o in outs],
        out_shape=outs,
        compiler_params=_cparams("parallel"),
        name="in_proj",
    )(x, mod.arr, win.arr, wgk2.arr, bgk.arr)


def _gla_kernel(qk_l, v_l, r_l, qk_c, v_c, r_c, g_ref, y_l, y_c, ht, oacc_l, oacc_c,
                *, nch_l, nch_c):
    c = GLA_C
    nh = GLA_HEADS
    ii4 = lax.broadcasted_iota(jnp.int32, (nh * c, c), 0) & (c - 1)
    jj4 = lax.broadcasted_iota(jnp.int32, (nh * c, c), 1)
    lane_qk = lax.broadcasted_iota(jnp.int32, (c, 128), 1) // (128 // nh)
    lane_v = lax.broadcasted_iota(jnp.int32, (c, 256), 1) // (256 // nh)
    st_mask = (lax.broadcasted_iota(jnp.int32, (256, 128), 0) // (256 // nh)
               == lax.broadcasted_iota(jnp.int32, (256, 128), 1) // (128 // nh))

    def chunk(qk, v, oacc, idx, fwd):
        dirn = 0 if fwd else 1
        pair4 = (jj4 <= ii4) if fwd else (jj4 > ii4)
        mid, last = (c // 2 - 1, c - 1) if fwd else (c // 2, 0)
        la0 = 256 if fwd else 384
        rows = pl.ds(pl.multiple_of(idx * c, c), c)
        q = qk[rows, 0:128]
        k = qk[rows, 128:256]
        b = qk[rows, la0:la0 + 128]
        b_mid = b[mid:mid + 1, :]
        b_last = b[last:last + 1, :]
        qe = q * jnp.exp(b - b_mid)
        ke = (k * jnp.exp(b_mid - b)).astype(BF16)
        kt = (k * jnp.exp(b_last - b)).astype(BF16)
        qa = (qe * jnp.exp(b_mid)).astype(BF16)
        qs = jnp.concatenate([jnp.where(lane_qk == h, qe, 0.0) for h in range(nh)],
                             axis=0).astype(BF16)
        att = jnp.where(pair4, _dot_nt(qs, ke), 0.0).astype(BF16)
        vv = v[rows, :]
        os_ = _dot(att, vv)
        o = _dot_nt(qa, ht[dirn].astype(BF16))
        for h in range(nh):
            o = o + jnp.where(lane_v == h, os_[h * c:(h + 1) * c, :], 0.0)
        oacc[dirn, rows, :] = o
        dht = _dot_tn(vv, kt)
        ht[dirn] = ht[dirn] * jnp.exp(b_last) + jnp.where(st_mask, dht, 0.0)

    def run(qk, v, oacc, nch):
        unroll = math.gcd(nch, GLA_UNROLL)

        def step(n, carry):
            for u in range(unroll):
                chunk(qk, v, oacc, n * unroll + u, True)
                chunk(qk, v, oacc, nch - 1 - (n * unroll + u), False)
            return carry

        lax.fori_loop(0, nch // unroll, step, 0)

    ht[...] = jnp.zeros_like(ht)
    run(qk_c, v_c, oacc_c, nch_c)
    run(qk_l, v_l, oacc_l, nch_l)

    hv = 256 // nh
    mh = jnp.where(lax.broadcasted_iota(jnp.int32, (256, 256), 0) // hv
                   == lax.broadcasted_iota(jnp.int32, (256, 256), 1) // hv, 1.0 / hv, 0.0).astype(BF16)
    g = g_ref[...]

    def finish(oacc, r, y, nch):
        unroll = math.gcd(nch, GLA_UNROLL)

        def step(n, carry):
            for u in range(unroll):
                rows = pl.ds(pl.multiple_of((n * unroll + u) * c, c), c)
                o = oacc[0, rows, :] + oacc[1, rows, :]
                sq = o * o
                sh = sq.astype(BF16)
                sl = (sq - sh.astype(F32)).astype(BF16)
                ms = _dot(sh, mh) + _dot(sl, mh)
                rr = r[rows, :].astype(F32)
                y[rows, :] = (o * lax.rsqrt(ms + EPS) * g * _silu(rr)).astype(BF16)
            return carry

        lax.fori_loop(0, nch // unroll, step, 0)

    finish(oacc_c, r_c, y_c, nch_c)
    finish(oacc_l, r_l, y_l, nch_l)


def _gla(bsz, t, tc, lat, ctx, gla_g):
    c = GLA_C
    nch_l, nch_c = t // c, tc // c

    def prep(arrs, seq):
        qkla, v, r = arrs
        return (qkla.reshape(bsz, seq, 512), v.reshape(bsz, seq, 256), r.reshape(bsz, seq, 256))

    def specs(seq):
        return [pl.BlockSpec((None, seq, 512), lambda b: (b, 0, 0)),
                pl.BlockSpec((None, seq, 256), lambda b: (b, 0, 0)),
                pl.BlockSpec((None, seq, 256), lambda b: (b, 0, 0))]

    y_l, y_c = pl.pallas_call(
        functools.partial(_gla_kernel, nch_l=nch_l, nch_c=nch_c),
        grid=(bsz,),
        in_specs=specs(t) + specs(tc) + [gla_g.spec()],
        out_specs=[pl.BlockSpec((None, t, 256), lambda b: (b, 0, 0)),
                   pl.BlockSpec((None, tc, 256), lambda b: (b, 0, 0))],
        out_shape=[jax.ShapeDtypeStruct((bsz, t, 256), BF16),
                   jax.ShapeDtypeStruct((bsz, tc, 256), BF16)],
        scratch_shapes=[pltpu.VMEM((2, 256, 128), F32), pltpu.VMEM((2, t, 256), F32),
                        pltpu.VMEM((2, tc, 256), F32)],
        compiler_params=_cparams("parallel"),
        name="gla",
    )(*prep(lat, t), *prep(ctx, tc), gla_g.arr)
    return y_l.reshape(bsz * t, 256), y_c.reshape(bsz * tc, 256)


POOL_PACK = 4


def _pool_kernel(x_ref, s_ref, ic_ref, pw_ref, ps_ref, o_ref, xp_ref, op_ref, *, ng, seq):
    g = pl.program_id(1)
    grp = 256 // ng
    rc = min(seq, 512)

    @pl.when(g == 0)
    def _():
        def pack(c, carry):
            rows = pl.ds(pl.multiple_of(c * rc, rc), rc)
            for gg in range(ng):
                xp_ref[gg, rows, :] = jnp.concatenate(
                    [x_ref[b, rows, gg * grp:(gg + 1) * grp] for b in range(POOL_PACK)], axis=-1)
            return carry

        lax.fori_loop(0, seq // rc, pack, 0)

    x = xp_ref[g]
    ic = ic_ref[...]
    m = _dot(s_ref[...], x) * jnp.concatenate([ic, ic], axis=-1)
    dlt = (m - x.astype(F32)).astype(BF16)
    op_ref[g] = (_dot(dlt, pw_ref[...]) * ps_ref[...]).astype(BF16)

    @pl.when(g == ng - 1)
    def _():
        def unpack(c, carry):
            rows = pl.ds(pl.multiple_of(c * rc, rc), rc)
            for b in range(POOL_PACK):
                o_ref[b, rows, :] = jnp.concatenate(
                    [op_ref[gg, rows, b * grp:(b + 1) * grp] for gg in range(ng)], axis=-1)
            return carry

        lax.fori_loop(0, seq // rc, unpack, 0)


def _window_bounds(pos, n, win):
    return jnp.clip(pos - win // 2, 0, n), jnp.clip(pos + (win - win // 2), 0, n)


def _pool_tables(seq, rows):
    mats, invs = [], []
    tok = jnp.arange(seq)
    for win in POOL_WINDOWS:
        if rows is None:
            lo, hi = _window_bounds(tok, seq, win)
            member = (tok[None, :] >= lo[:, None]) & (tok[None, :] < hi[:, None])
            cnt = hi - lo
        else:
            rr, cc = tok // GRID_W, tok % GRID_W
            lo_r, hi_r = _window_bounds(rr, rows, win)
            lo_c, hi_c = _window_bounds(cc, GRID_W, win)
            member = ((rr[None, :] >= lo_r[:, None]) & (rr[None, :] < hi_r[:, None])
                      & (cc[None, :] >= lo_c[:, None]) & (cc[None, :] < hi_c[:, None]))
            cnt = (hi_r - lo_r) * (hi_c - lo_c)
        mats.append(member.astype(BF16))
        invs.append(jnp.broadcast_to((1.0 / cnt.astype(F32))[:, None], (seq, LANES)))
    return jnp.stack(mats), jnp.stack(invs)


def _pool(bsz, seq, xp, tables, pw_bd, ps_t, l):
    smat, inv = tables
    ng = len(POOL_WINDOWS)
    nq = bsz // POOL_PACK
    blk = pl.BlockSpec((POOL_PACK, seq, 256), lambda q, g: (q, 0, 0))
    out = pl.pallas_call(
        functools.partial(_pool_kernel, ng=ng, seq=seq),
        grid=(nq, ng),
        in_specs=[blk,
                  pl.BlockSpec((None, seq, seq), lambda q, g: (g, 0, 0)),
                  pl.BlockSpec((None, seq, LANES), lambda q, g: (g, 0, 0)),
                  pl.BlockSpec((None, None, 256, 256), lambda q, g: (l, g, 0, 0)),
                  pl.BlockSpec((None, None, 1, 256), lambda q, g: (l, g, 0, 0))],
        out_specs=blk,
        out_shape=jax.ShapeDtypeStruct((bsz, seq, 256), BF16),
        scratch_shapes=[pltpu.VMEM((ng, seq, 256), BF16), pltpu.VMEM((ng, seq, 256), BF16)],
        compiler_params=_cparams("arbitrary", "arbitrary"),
        name="pool",
    )(xp.reshape(bsz, seq, 256), smat, inv, pw_bd, ps_t)
    return out.reshape(bsz * seq, 256)


def _fourier_kernel(z_ref, cb_ref, sb_ref, ct_ref, st_ref, o_ref, *, scale):
    z = z_ref[...]
    zc = _dot(z, cb_ref[...]).astype(BF16)
    zs = _dot(z, sb_ref[...]).astype(BF16)
    y = _dot(ct_ref[...], zc) - _dot(st_ref[...], zs)
    o_ref[...] = (y * scale).astype(BF16)


def _dft_tables(n):
    k = jnp.arange(n, dtype=jnp.int32)
    ang = ((k[:, None] * k[None, :]) % n).astype(F32) * (2.0 * math.pi / n)
    return jnp.cos(ang), jnp.sin(ang)


def _fourier_chan_tables(ng):
    grp = 256 // ng
    cg, sg = _dft_tables(grp)
    eye = jnp.eye(ng, dtype=F32)
    return jnp.kron(eye, cg).astype(BF16), jnp.kron(eye, sg).astype(BF16)


def _fourier(bsz, seq, z, chan_tabs, seq_tabs):
    cb, sb = chan_tabs
    ct, st = seq_tabs
    grp = cb.shape[0] // 4
    out = pl.pallas_call(
        functools.partial(_fourier_kernel, scale=1.0 / math.sqrt(seq * grp)),
        grid=(bsz,),
        in_specs=[pl.BlockSpec((None, seq, 256), lambda b: (b, 0, 0)),
                  _resident((256, 256)), _resident((256, 256)),
                  _resident((seq, seq)), _resident((seq, seq))],
        out_specs=pl.BlockSpec((None, seq, 256), lambda b: (b, 0, 0)),
        out_shape=jax.ShapeDtypeStruct((bsz, seq, 256), BF16),
        compiler_params=_cparams("parallel"),
        name="fourier",
    )(z.reshape(bsz, seq, 256), cb, sb, ct, st)
    return out.reshape(bsz * seq, 256)


def _sgu_kernel(s_ref, lg_ref, lb_ref, w_ref, bm_ref, o_ref, *, n_chunks, ng):
    u = _gelu_tanh(s_ref[:, 0:256].astype(F32))
    v = _gelu_tanh(s_ref[:, 256:512].astype(F32))
    vb = (_ln(v) * lg_ref[...] + lb_ref[...]).astype(BF16)
    lane_g = lax.broadcasted_iota(jnp.int32, (SGU_CHUNK, 256), 1) // (256 // ng)
    for ch in range(n_chunks):
        rows = slice(ch * SGU_CHUNK, (ch + 1) * SGU_CHUNK)
        vc = vb[rows, :]
        s = bm_ref[...]
        for g in range(ng):
            s = s + jnp.where(lane_g == g, _dot(w_ref[g], vc), 0.0)
        o_ref[rows, :] = (u[rows, :] * s).astype(BF16)


def _sgu(st, sgu_in, lg, lb, w, bm):
    ng = w.shape[0]
    tm = st.tm
    return pl.pallas_call(
        functools.partial(_sgu_kernel, n_chunks=tm // SGU_CHUNK, ng=ng),
        grid=(st.n_tiles,),
        in_specs=[st.tok_spec(512), lg.spec(), lb.spec(), w.spec(), bm.spec()],
        out_specs=st.tok_spec(256),
        out_shape=jax.ShapeDtypeStruct((st.n_tok, 256), BF16),
        compiler_params=_cparams("parallel"),
        name="sgu",
    )(sgu_in, lg.arr, lb.arr, w.arr, bm.arr)


MERGE_COLS = 256


def _merge_kernel(x_ref, h_ref, y0, y1, y2, y3, mod_ref, wg_ref, wb_ref, wo_ref, lg_ref, lb_ref,
                  o_ref, *, d, alpha):
    h = h_ref[...]
    ys = (y0[...], y1[...], y2[...], y3[...])
    parts = []
    for cb in range(d // MERGE_COLS):
        cols = slice(cb * MERGE_COLS, (cb + 1) * MERGE_COLS)
        m = None
        for i in range(4):
            t = _sigmoid(_dot(h, wg_ref[i, :, cols])) * _dot(ys[i], wb_ref[i, :, cols])
            m = t if m is None else m + t
        parts.append(m.astype(BF16))
    y = _dot(jnp.concatenate(parts, axis=-1), wo_ref[...])
    gate = mod_ref[0, :, 2 * d:3 * d]
    xn = alpha * x_ref[...] + gate * y
    o_ref[...] = _ln(xn) * lg_ref[...] + lb_ref[...]


def _merge(st, x, h, ys, mod, wg, wb, wo, lg, lb, alpha):
    d = x.shape[1]
    return pl.pallas_call(
        functools.partial(_merge_kernel, d=d, alpha=alpha),
        grid=(st.n_tiles,),
        in_specs=[st.tok_spec(d), st.tok_spec(d)] + [st.tok_spec(256)] * 4
        + [st.mod_spec(mod), wg.spec(), wb.spec(), wo.spec(), lg.spec(), lb.spec()],
        out_specs=st.tok_spec(d),
        out_shape=jax.ShapeDtypeStruct((st.n_tok, d), F32),
        compiler_params=_cparams("parallel"),
        name="merge",
    )(x, h, *ys, mod.arr, wg.arr, wb.arr, wo.arr, lg.arr, lb.arr)


def _ff_blocks(ff):
    if ff <= 1536 or ff % 256:
        return [(0, ff)]
    first = (ff // 512 + (1 if (ff // 256) % 2 else 0)) * 256
    return [(0, first), (first, ff)]


def _ffn_kernel(x_ref, mod_ref, wg_ref, wu_ref, wd_ref, lg_ref, lb_ref, o_ref, *, d, alpha, blocks):
    x = x_ref[...]
    shift = mod_ref[0, :, 3 * d:4 * d]
    scale = mod_ref[0, :, 4 * d:5 * d]
    gate = mod_ref[0, :, 5 * d:6 * d]
    hb = (_ln(x) * (1.0 + scale) + shift).astype(BF16)
    y = None
    for lo, hi in blocks:
        a = (_silu(_dot(hb, wg_ref[:, lo:hi])) * _dot(hb, wu_ref[:, lo:hi])).astype(BF16)
        t = _dot(a, wd_ref[lo:hi, :])
        y = t if y is None else y + t
    xn = alpha * x + gate * y
    o_ref[...] = _ln(xn) * lg_ref[...] + lb_ref[...]


def _ffn(st, x, mod, wg, wu, wd, lg, lb, alpha):
    d = x.shape[1]
    ff = wg.shape[1]
    return pl.pallas_call(
        functools.partial(_ffn_kernel, d=d, alpha=alpha, blocks=_ff_blocks(ff)),
        grid=(st.n_tiles,),
        in_specs=[st.tok_spec(d), st.mod_spec(mod), wg.spec(), wu.spec(), wd.spec(),
                  lg.spec(), lb.spec()],
        out_specs=st.tok_spec(d),
        out_shape=jax.ShapeDtypeStruct((st.n_tok, d), F32),
        compiler_params=_cparams("parallel"),
        name="ffn",
    )(x, mod.arr, wg.arr, wu.arr, wd.arr, lg.arr, lb.arr)


MOE_PIECE = 16
MOE_ROW_TILE = 512
MOE_META = LANES


def _moe_local_rows(tm, n_exp):
    return -(-(TOP_K * tm + n_exp * (MOE_PIECE - 1)) // LANES) * LANES


ROUTE_ROWS = SUBLANES


def _router_kernel(x_ref, mod_ref, wr_ref, hb_ref, rt_ref, cnt_ref, *, d, n_exp, tm):
    i = pl.program_id(0)
    x = x_ref[...]
    shift = mod_ref[0, :, 3 * d:4 * d]
    scale = mod_ref[0, :, 4 * d:5 * d]
    h = _ln(x) * (1.0 + scale) + shift
    hb = h.astype(BF16)
    hb_ref[...] = hb

    wr = wr_ref[...]
    wrh = wr.astype(BF16)
    wrl = (wr - wrh.astype(F32)).astype(BF16)
    hl = (h - hb.astype(F32)).astype(BF16)
    logits = _dot(hb, wrh) + (_dot(hb, wrl) + _dot(hl, wrh))
    lt = logits.T[0:n_exp, :]
    sub = lax.broadcasted_iota(jnp.int32, (n_exp, tm), 0).astype(F32)
    neg = jnp.float32(-jnp.inf)
    v1 = jnp.max(lt, axis=0, keepdims=True)
    i1 = jnp.min(jnp.where(lt == v1, sub, float(n_exp)), axis=0, keepdims=True)
    rest = jnp.where(sub == i1, neg, lt)
    v2 = jnp.max(rest, axis=0, keepdims=True)
    i2 = jnp.min(jnp.where(rest == v2, sub, float(n_exp)), axis=0, keepdims=True)
    e2 = jnp.exp(v2 - v1)
    w1 = 1.0 / (1.0 + e2)
    w2 = e2 / (1.0 + e2)
    oht = jnp.where(sub == i1, 1.0, jnp.where(sub == i2, 1.0, 0.0))
    cnt_col = jnp.sum(oht, axis=1, keepdims=True)
    for e in range(n_exp):
        cnt_ref[i, e] = cnt_col[e, 0].astype(jnp.int32)
    rec = lax.broadcasted_iota(jnp.int32, (ROUTE_ROWS, tm), 0)
    rt_ref[...] = jnp.where(rec == 0, i1, jnp.where(rec == 1, i2, jnp.where(
        rec == 2, w1, jnp.where(rec == 3, w2, 0.0))))


def _dispatch_kernel(seg_ref, zf_ref, hb_ref, rt_ref, u_ref, xs_hbm, meta_ref,
                     sorted_ref, zero_ref, ztile_ref, sem, *, d, n_exp, tm, ls, n_tiles, tf):
    i = pl.program_id(0)
    hb = hb_ref[...]
    i1 = rt_ref[0:1, :]
    i2 = rt_ref[1:2, :]
    w1 = rt_ref[2:3, :]
    w2 = rt_ref[3:4, :]
    sub = lax.broadcasted_iota(jnp.int32, (n_exp, tm), 0).astype(F32)
    oh1 = sub == i1
    oh2 = sub == i2
    oht = jnp.where(oh1, 1.0, jnp.where(oh2, 1.0, 0.0))
    cumt = _dot(oht.astype(BF16), u_ref[...])
    cnts = [seg_ref[i, n_exp + e] for e in range(n_exp)]
    npc = [(cn + (MOE_PIECE - 1)) // MOE_PIECE for cn in cnts]
    los = []
    lo = jnp.int32(0)
    for e in range(n_exp):
        los.append(lo)
        lo = lo + npc[e] * MOE_PIECE
    lo_b = jnp.zeros((n_exp, tm), F32)
    for e in range(n_exp):
        lo_b = jnp.where(sub == float(e), los[e].astype(F32), lo_b)
    lpt = cumt + lo_b
    lp1 = jnp.sum(jnp.where(oh1, lpt, 0.0), axis=0, keepdims=True)
    lp2 = jnp.sum(jnp.where(oh2, lpt, 0.0), axis=0, keepdims=True)

    row = lax.broadcasted_iota(jnp.int32, (ls, tm), 0).astype(F32)
    m1 = row == lp1
    m2 = row == lp2
    perm = jnp.where(m1, 1.0, jnp.where(m2, 1.0, 0.0)).astype(BF16)
    gcol = jnp.sum(jnp.where(m1, w1, jnp.where(m2, w2, 0.0)), axis=1, keepdims=True)
    ghi = gcol.astype(BF16).astype(F32)
    r1 = gcol - ghi
    gmid = r1.astype(BF16).astype(F32)
    glo = r1 - gmid
    lane = lax.broadcasted_iota(jnp.int32, (ls, MOE_META), 1)
    gate_lanes = jnp.where(
        lane == 0, ghi, jnp.where(lane == 1, gmid, jnp.where(lane == 2, glo, 0.0))).astype(BF16)

    def piece(src_ref, src_row, dst_row, s):
        return pltpu.make_async_copy(
            src_ref.at[pl.ds(pl.multiple_of(src_row, MOE_PIECE), MOE_PIECE), :],
            xs_hbm.at[pl.ds(pl.multiple_of(dst_row, MOE_PIECE), MOE_PIECE), :], sem.at[s])

    def drain(tile):
        n = jnp.int32(0)
        for e in range(n_exp):
            n = n + (seg_ref[tile, n_exp + e] + (MOE_PIECE - 1)) // MOE_PIECE

        def wait(j, carry):
            piece(sorted_ref, 0, 0, 0).wait()
            return carry

        lax.fori_loop(0, n, wait, 0)

    @pl.when(i >= 1)
    def _():
        drain(i - 1)

    sorted_ref[:, 0:d] = _dot(perm, hb).astype(BF16)
    sorted_ref[:, d:d + MOE_META] = gate_lanes

    rows8 = jnp.where(sub == 0.0, lp1, jnp.where(sub == 1.0, lp2, 0.0))
    padded = jnp.concatenate([rows8, jnp.zeros((LANES - n_exp, tm), F32)], axis=0)
    meta_ref[...] = padded.T

    for e in range(n_exp):
        base = seg_ref[i, e]

        def start(j, carry, e=e, base=base):
            piece(sorted_ref, los[e] + j * MOE_PIECE, base + j * MOE_PIECE, 0).start()
            return carry

        lax.fori_loop(0, npc[e], start, 0)

    @pl.when(i == n_tiles - 1)
    def _():
        drain(i)
        zero_ref[...] = jnp.zeros_like(zero_ref)
        ztile_ref[...] = jnp.zeros_like(ztile_ref)
        n_zero = jnp.int32(0)
        for e in range(n_exp):
            z0 = zf_ref[2 * e]
            nz = zf_ref[2 * e + 1]

            def zstart(j, carry, z0=z0):
                piece(zero_ref, 0, z0 + j * MOE_PIECE, 1).start()
                return carry

            lax.fori_loop(0, nz, zstart, 0)
            n_zero = n_zero + nz

        def zwait(j, carry):
            piece(zero_ref, 0, 0, 1).wait()
            return carry

        lax.fori_loop(0, n_zero, zwait, 0)

        def tile_copy(t):
            return pltpu.make_async_copy(
                ztile_ref, xs_hbm.at[pl.ds(pl.multiple_of(t * tf, tf), tf), :], sem.at[1])

        t0 = zf_ref[2 * n_exp]
        t1 = zf_ref[2 * n_exp + 1]

        def tstart(t, carry):
            tile_copy(t).start()
            return carry

        def twait(t, carry):
            tile_copy(t).wait()
            return carry

        lax.fori_loop(t0, t1, tstart, 0)
        lax.fori_loop(t0, t1, twait, 0)


def _expert_kernel(te_ref, na_ref, xs_ref, wg_ref, wu_ref, wd_ref, ys_ref, *, d, blocks):
    active = pl.program_id(0) < na_ref[0]

    @pl.when(jnp.logical_not(active))
    def _():
        ys_ref[...] = jnp.zeros_like(ys_ref)

    @pl.when(active)
    def _():
        hb = xs_ref[:, 0:d]
        gate = jnp.sum(xs_ref[:, d:d + MOE_META].astype(F32), axis=-1, keepdims=True)
        y = None
        for lo, hi in blocks:
            a = (_silu(_dot(hb, wg_ref[:, lo:hi])) * _dot(hb, wu_ref[:, lo:hi])).astype(BF16)
            t = _dot(a, wd_ref[lo:hi, :])
            y = t if y is None else y + t
        ys_ref[...] = (y * gate).astype(BF16)


def _combine_kernel(seg_ref, x_ref, mod_ref, meta_ref, ys_hbm, lg_ref, lb_ref, o_ref, ysl_ref, sem,
                    *, d, alpha, n_exp, tm, ls, n_tiles):
    i = pl.program_id(0)

    def piece(src_row, dst_row, s):
        return pltpu.make_async_copy(
            ys_hbm.at[pl.ds(pl.multiple_of(src_row, MOE_PIECE), MOE_PIECE), :],
            ysl_ref.at[s, pl.ds(pl.multiple_of(dst_row, MOE_PIECE), MOE_PIECE), :], sem.at[s])

    def n_pieces(tile):
        n = jnp.int32(0)
        for e in range(n_exp):
            n = n + (seg_ref[tile, n_exp + e] + (MOE_PIECE - 1)) // MOE_PIECE
        return n

    def fetch(tile):
        s = tile % 2
        lo = jnp.int32(0)
        for e in range(n_exp):
            base = seg_ref[tile, e]
            npc = (seg_ref[tile, n_exp + e] + (MOE_PIECE - 1)) // MOE_PIECE

            def start(j, carry, base=base, lo=lo):
                piece(base + j * MOE_PIECE, lo + j * MOE_PIECE, s).start()
                return carry

            lax.fori_loop(0, npc, start, 0)
            lo = lo + npc * MOE_PIECE

        def clear(j, carry):
            ysl_ref[s, pl.ds(pl.multiple_of(j * MOE_PIECE, MOE_PIECE), MOE_PIECE), :] = jnp.zeros(
                (MOE_PIECE, d), BF16)
            return carry

        lax.fori_loop(lo // MOE_PIECE, ls // MOE_PIECE, clear, 0)

    @pl.when(i == 0)
    def _():
        fetch(i)

    @pl.when(i + 1 < n_tiles)
    def _():
        fetch(i + 1)

    slot = i % 2

    def wait(j, carry):
        piece(0, 0, slot).wait()
        return carry

    lax.fori_loop(0, n_pieces(i), wait, 0)

    lp1 = meta_ref[:, 0:1]
    lp2 = meta_ref[:, 1:2]
    col = lax.broadcasted_iota(jnp.int32, (tm, ls), 1).astype(F32)
    perm_t = jnp.where(col == lp1, 1.0, jnp.where(col == lp2, 1.0, 0.0)).astype(BF16)
    y = _dot(perm_t, ysl_ref[slot])
    gate = mod_ref[0, :, 5 * d:6 * d]
    xn = alpha * x_ref[...] + gate * y
    o_ref[...] = _ln(xn) * lg_ref[...] + lb_ref[...]


def _moe(st, x, mod, wr_pad, wg, wu, wd, lg, lb, alpha):
    d = x.shape[1]
    n_exp, _, ff = wg.shape
    (li,) = wg.idx
    tm, n_tiles, n_tok = st.tm, st.n_tiles, st.n_tok
    ls = _moe_local_rows(tm, n_exp)
    tf = MOE_ROW_TILE
    dx = d + MOE_META
    i32 = jnp.int32

    hb, rt, cnt = pl.pallas_call(
        functools.partial(_router_kernel, d=d, n_exp=n_exp, tm=tm),
        grid=(n_tiles,),
        in_specs=[st.tok_spec(d), st.mod_spec(mod), wr_pad.spec()],
        out_specs=[st.tok_spec(d),
                   pl.BlockSpec((None, ROUTE_ROWS, tm), lambda i: (i, 0, 0)),
                   pl.BlockSpec(memory_space=pltpu.SMEM)],
        out_shape=[jax.ShapeDtypeStruct((n_tok, d), BF16),
                   jax.ShapeDtypeStruct((n_tiles, ROUTE_ROWS, tm), F32),
                   jax.ShapeDtypeStruct((n_tiles, n_exp), i32)],
        compiler_params=_cparams("arbitrary"),
        name="moe_router",
    )(x, mod.arr, wr_pad.arr)

    padded = -(-cnt // MOE_PIECE) * MOE_PIECE
    per_e = jnp.sum(padded, axis=0)
    per_e_full = -(-per_e // tf) * tf
    region_end = jnp.cumsum(per_e_full)
    region_start = region_end - per_e_full
    base = region_start[None, :] + jnp.cumsum(padded, axis=0) - padded
    seg = jnp.concatenate([base, cnt], axis=1).astype(i32)
    n_row_tiles = -(-(TOP_K * n_tok + n_tiles * n_exp * (MOE_PIECE - 1) + n_exp * (tf - 1)) // tf)
    tile_end = region_end // tf
    n_act = tile_end[-1]
    zfill = jnp.concatenate([
        jnp.stack([region_start + per_e, (per_e_full - per_e) // MOE_PIECE], axis=1).reshape(-1),
        jnp.stack([n_act, jnp.asarray(n_row_tiles, i32)])]).astype(i32)
    t_idx = jnp.minimum(jnp.arange(n_row_tiles, dtype=i32), n_act - 1)
    t_e = jnp.sum(t_idx[:, None] >= tile_end[None, :], axis=1).astype(i32)
    upper = (jnp.arange(tm)[:, None] < jnp.arange(tm)[None, :]).astype(BF16)

    xs, meta = pl.pallas_call(
        functools.partial(_dispatch_kernel, d=d, n_exp=n_exp, tm=tm, ls=ls, n_tiles=n_tiles, tf=tf),
        grid_spec=pltpu.PrefetchScalarGridSpec(
            num_scalar_prefetch=2,
            grid=(n_tiles,),
            in_specs=[st.tok_spec(d),
                      pl.BlockSpec((None, ROUTE_ROWS, tm), lambda i, *_: (i, 0, 0)),
                      _resident((tm, tm))],
            out_specs=[pl.BlockSpec(memory_space=pl.ANY), st.tok_spec(LANES)],
            scratch_shapes=[pltpu.VMEM((ls, dx), BF16), pltpu.VMEM((MOE_PIECE, dx), BF16),
                            pltpu.VMEM((tf, dx), BF16), pltpu.SemaphoreType.DMA((2,))],
        ),
        out_shape=[jax.ShapeDtypeStruct((n_row_tiles * tf, dx), BF16),
                   jax.ShapeDtypeStruct((n_tok, LANES), F32)],
        compiler_params=_cparams("arbitrary"),
        name="moe_dispatch",
    )(seg, zfill, hb, rt, upper)

    ys = pl.pallas_call(
        functools.partial(_expert_kernel, d=d, blocks=_ff_blocks(ff)),
        grid_spec=pltpu.PrefetchScalarGridSpec(
            num_scalar_prefetch=2,
            grid=(n_row_tiles,),
            in_specs=[pl.BlockSpec((tf, dx), lambda i, te, na: (i, 0)),
                      pl.BlockSpec((None, None, d, ff), lambda i, te, na: (li, te[i], 0, 0)),
                      pl.BlockSpec((None, None, d, ff), lambda i, te, na: (li, te[i], 0, 0)),
                      pl.BlockSpec((None, None, ff, d), lambda i, te, na: (li, te[i], 0, 0))],
            out_specs=pl.BlockSpec((tf, d), lambda i, te, na: (i, 0)),
        ),
        out_shape=jax.ShapeDtypeStruct((n_row_tiles * tf, d), BF16),
        compiler_params=_cparams("arbitrary"),
        name="moe_experts",
    )(t_e, n_act.reshape(1).astype(i32), xs, wg.arr, wu.arr, wd.arr)

    return pl.pallas_call(
        functools.partial(_combine_kernel, d=d, alpha=alpha, n_exp=n_exp, tm=tm, ls=ls,
                          n_tiles=n_tiles),
        grid_spec=pltpu.PrefetchScalarGridSpec(
            num_scalar_prefetch=1,
            grid=(n_tiles,),
            in_specs=[st.tok_spec(d), st.mod_spec(mod), st.tok_spec(LANES),
                      pl.BlockSpec(memory_space=pl.ANY), lg.spec(), lb.spec()],
            out_specs=st.tok_spec(d),
            scratch_shapes=[pltpu.VMEM((2, ls, d), BF16), pltpu.SemaphoreType.DMA((2,))],
        ),
        out_shape=jax.ShapeDtypeStruct((n_tok, d), F32),
        compiler_params=_cparams("arbitrary"),
        name="moe_combine",
    )(seg, x, mod.arr, meta, ys, lg.arr, lb.arr)


def kernel(x, c, ctx, c_ctx, w_mod, b_mod, w_in, w_gk2, b_gk, gla_g, pool_w, pool_scale, sgu_ln_g, sgu_ln_b, sgu_w, sgu_b, w_gate, w_branch, w_o, ln_g, ln_b, w_ffn_gate, w_ffn_up, w_ffn_down, w_router, w_exp_gate, w_exp_up, w_exp_down):
    bsz, t, d = x.shape
    tc = ctx.shape[1]
    depth = w_mod.shape[0]
    n6 = w_mod.shape[2]
    alpha = (2 * depth) ** 0.25
    rows = t // GRID_W
    ng = len(POOL_WINDOWS)
    gk_rank = w_gk2.shape[2]
    assert d == 1024 and w_in.shape[2] == 1792 + 2 * gk_rank and 2 * gk_rank <= LANES
    assert t % GLA_C == 0 and tc % GLA_C == 0 and bsz % POOL_PACK == 0

    n_rows = -(-(bsz + 1) // SUBLANES) * SUBLANES
    cvec = jnp.zeros((n_rows, d), F32).at[:bsz].set(c).at[bsz].set(c_ctx)
    mod_all = _modulation(cvec, w_mod, b_mod)

    lat = _lat_stream(bsz, t)
    cst = _ctx_stream(bsz, tc)
    xl = x.reshape(bsz * t, d)
    xc = ctx.reshape(bsz * tc, d)

    pool_tab_l = _pool_tables(t, rows)
    pool_tab_c = _pool_tables(tc, None)
    chan_tabs = _fourier_chan_tables(ng)
    dft_l = tuple(a.astype(BF16) for a in _dft_tables(t))
    dft_c = tuple(a.astype(BF16) for a in _dft_tables(tc))

    gk0 = 768
    gk1 = gk0 + 2 * gk_rank

    mod4 = mod_all.reshape(depth, n_rows, 1, n6)
    win_all = jnp.concatenate(
        [w_in[:, :, :gk0], w_in[:, :, gk1:], w_in[:, :, gk0:gk1],
         jnp.zeros((depth, d, LANES - 2 * gk_rank), F32)], axis=2).astype(BF16)
    wgk2_all = jnp.zeros((depth, LANES, 256), F32)
    wgk2_all = wgk2_all.at[:, 0:gk_rank, 0:128].set(w_gk2[:, 0])
    wgk2_all = wgk2_all.at[:, gk_rank:2 * gk_rank, 128:256].set(w_gk2[:, 1]).astype(BF16)
    bgk_all = b_gk.reshape(depth, 1, 256)
    gla_g_all = gla_g.reshape(depth, 1, 256)
    grp = 256 // ng
    pw_bd_all = jnp.einsum("ab,lgcd->lgacbd", jnp.eye(POOL_PACK, dtype=F32), pool_w).reshape(
        depth, ng, 256, 256).astype(BF16)
    ps_all = jnp.tile(pool_scale.reshape(depth, ng, 1, grp), (1, 1, 1, POOL_PACK))
    sgu_lg_all = sgu_ln_g.reshape(depth, 1, 256)
    sgu_lb_all = sgu_ln_b.reshape(depth, 1, 256)
    sgu_w_all = sgu_w.astype(BF16)
    sgu_bm_all = jnp.repeat(jnp.swapaxes(sgu_b, 1, 2), grp, axis=2)
    wg_all = w_gate.astype(BF16)
    wb_all = w_branch.astype(BF16)
    wo_all = w_o.astype(BF16)
    ln_g4 = ln_g.reshape(depth, 2, 1, d)
    ln_b4 = ln_b.reshape(depth, 2, 1, d)
    ffn_all = (w_ffn_gate.astype(BF16), w_ffn_up.astype(BF16), w_ffn_down.astype(BF16))
    n_exp = w_router.shape[2]
    wr_all = jnp.zeros((w_router.shape[0], d, LANES), F32).at[:, :, :n_exp].set(w_router)
    exp_all = (w_exp_gate.astype(BF16), w_exp_up.astype(BF16), w_exp_down.astype(BF16))

    for l in range(depth):
        last = l == depth - 1
        mod = _Sel(mod4, l)
        win, wgk2, bgk = _Sel(win_all, l), _Sel(wgk2_all, l), _Sel(bgk_all, l)
        wg, wb, wo = _Sel(wg_all, l), _Sel(wb_all, l), _Sel(wo_all, l)
        lg0, lb0 = _Sel(ln_g4, l, 0), _Sel(ln_b4, l, 0)
        lg1, lb1 = _Sel(ln_g4, l, 1), _Sel(ln_b4, l, 1)

        hl, qk_l, v_l, r_l, p_l, f_l, s_l = _inproj(lat, xl, mod, win, wgk2, bgk)
        hc, qk_c, v_c, r_c, p_c, f_c, s_c = _inproj(cst, xc, mod, win, wgk2, bgk)
        ya_l, ya_c = _gla(bsz, t, tc, (qk_l, v_l, r_l), (qk_c, v_c, r_c), _Sel(gla_g_all, l))
        sg = (_Sel(sgu_lg_all, l), _Sel(sgu_lb_all, l), _Sel(sgu_w_all, l), _Sel(sgu_bm_all, l))
        yb_l = _pool(bsz, t, p_l, pool_tab_l, pw_bd_all, ps_all, l)
        yc_l = _fourier(bsz, t, f_l, chan_tabs, dft_l)
        yd_l = _sgu(lat, s_l, *sg)
        xl = _merge(lat, xl, hl, (ya_l, yb_l, yc_l, yd_l), mod, wg, wb, wo, lg0, lb0, alpha)
        if not last:
            yb_c = _pool(bsz, tc, p_c, pool_tab_c, pw_bd_all, ps_all, l)
            yc_c = _fourier(bsz, tc, f_c, chan_tabs, dft_c)
            yd_c = _sgu(cst, s_c, *sg)
            xc = _merge(cst, xc, hc, (ya_c, yb_c, yc_c, yd_c), mod, wg, wb, wo, lg0, lb0, alpha)

        i = l // 2
        if l % 2 == 0:
            fw = tuple(_Sel(w, i) for w in ffn_all)
            xl = _ffn(lat, xl, mod, *fw, lg1, lb1, alpha)
            if not last:
                xc = _ffn(cst, xc, mod, *fw, lg1, lb1, alpha)
        else:
            ew = tuple(_Sel(w, i) for w in exp_all)
            xl = _moe(lat, xl, mod, _Sel(wr_all, i), *ew, lg1, lb1, alpha)
            if not last:
                xc = _moe(cst, xc, mod, _Sel(wr_all, i), *ew, lg1, lb1, alpha)
    return xl.reshape(bsz, t, d)
```
